```python
import jax, jax.numpy as jnp
from jax import lax
import numpy as np

D_MODEL = 4096
BATCH = 4
SEQ = 2048
DEPTH = 1

N_META = 16
HEAD_DIM = 128
ATTN_HEADS = D_MODEL // 256
ATTN_WIDTH = ATTN_HEADS * HEAD_DIM
CONV_WIDTH = D_MODEL - ATTN_WIDTH
MIX_WIDTH = ATTN_WIDTH + CONV_WIDTH
CONV_KERNEL = 31
IN_COLS = 3 * ATTN_WIDTH + ATTN_HEADS + 2 * CONV_WIDTH
N_GROUPS = 8
EXPERTS_PER_GROUP = 8
N_EXPERTS = N_GROUPS * EXPERTS_PER_GROUP
TOP_K = 2
D_FF_EXPERT = D_MODEL // 8
Q_BLOCK = 128
RMS_EPS = 1e-6
LN_EPS = 1e-5
FORGET_BIAS_MEAN = 4.0

kernel_name = "hymba_fox_conformer_hmoe_block"


def rmsnorm(x, g):
    xf = x.astype(jnp.float32)
    y = xf * lax.rsqrt(jnp.mean(xf * xf, axis=-1, keepdims=True) + RMS_EPS)
    return (y * g.astype(jnp.float32)).astype(x.dtype)


def forgetting_attention(q, k, v, logf):
    L = q.shape[1]
    scale = HEAD_DIM ** -0.5
    c = jnp.cumsum(logf, axis=1)
    c_bh = jnp.transpose(c, (0, 2, 1))
    bounds = [(0, N_META)] + [(s, min(s + Q_BLOCK, L)) for s in range(N_META, L, Q_BLOCK)]
    outs = []
    for qs, qe in bounds:
        qb = q[:, qs:qe]
        kb = k[:, :qe]
        vb = v[:, :qe]
        s = jnp.einsum('bqhd,bkhd->bhqk', qb, kb).astype(jnp.float32) * scale
        decay = c_bh[:, :, qs:qe, None] - c_bh[:, :, None, :qe]
        causal = jnp.arange(qs, qe)[:, None] >= jnp.arange(qe)[None, :]
        s = jnp.where(causal, s + decay, -jnp.inf)
        p = jax.nn.softmax(s, axis=-1).astype(vb.dtype)
        outs.append(jnp.einsum('bhqk,bkhd->bqhd', p, vb))
    return jnp.concatenate(outs, axis=1)


def conformer_conv(u, conv_w, conv_b, ln_g, ln_b):
    a, g = jnp.split(u, 2, axis=-1)
    h = a * jax.nn.sigmoid(g)
    w = conv_w.astype(h.dtype)[:, None, :]
    h = lax.conv_general_dilated(h, w, window_strides=(1,), padding=[(CONV_KERNEL - 1, 0)],
                                 dimension_numbers=('NWC', 'WIO', 'NWC'),
                                 feature_group_count=CONV_WIDTH) + conv_b.astype(h.dtype)
    hf = h.astype(jnp.float32)
    mu = jnp.mean(hf, axis=-1, keepdims=True)
    var = jnp.mean(jnp.square(hf - mu), axis=-1, keepdims=True)
    hf = (hf - mu) * lax.rsqrt(var + LN_EPS) * ln_g.astype(jnp.float32) + ln_b.astype(jnp.float32)
    return jax.nn.silu(hf).astype(u.dtype)


def hierarchical_moe(x, w_rg, b_rg, w_re, b_re, w_gate, w_up, w_down):
    B, L, D = x.shape
    xt = x.reshape(B * L, D)
    g_prob = jax.nn.softmax((xt @ w_rg).astype(jnp.float32) + b_rg.astype(jnp.float32), axis=-1)
    g_w, g_idx = lax.top_k(g_prob, 1)
    e_logits = ((xt @ w_re).astype(jnp.float32) + b_re.astype(jnp.float32)).reshape(-1, N_GROUPS, EXPERTS_PER_GROUP)
    e_in = jnp.take_along_axis(e_logits, g_idx[:, :, None], axis=1)[:, 0]
    e_w, e_idx = lax.top_k(jax.nn.softmax(e_in, axis=-1), TOP_K)
    e_w = e_w / jnp.sum(e_w, axis=-1, keepdims=True)
    expert_gate = jnp.sum(jax.nn.one_hot(e_idx, EXPERTS_PER_GROUP, dtype=jnp.float32) * e_w[..., None], axis=1)
    group_gate = jax.nn.one_hot(g_idx[:, 0], N_GROUPS, dtype=jnp.float32) * g_w
    gate = (group_gate[:, :, None] * expert_gate[:, None, :]).astype(x.dtype)
    wg = w_gate.reshape(N_GROUPS, EXPERTS_PER_GROUP, D, D_FF_EXPERT)
    wu = w_up.reshape(N_GROUPS, EXPERTS_PER_GROUP, D, D_FF_EXPERT)
    wd = w_down.reshape(N_GROUPS, EXPERTS_PER_GROUP, D_FF_EXPERT, D)
    out = jnp.zeros_like(xt)
    for gi in range(N_GROUPS):
        h = jax.nn.silu(jnp.einsum('td,edf->tef', xt, wg[gi])) * jnp.einsum('td,edf->tef', xt, wu[gi])
        out = out + jnp.einsum('tef,efd->td', h * gate[:, gi, :, None], wd[gi])
    return out.reshape(B, L, D)


def setup_inputs(seed: int = 0) -> dict:
    key = jax.random.key(seed)
    ks = jax.random.split(key, 20)
    f32 = jnp.float32
    nrm = lambda k, shape, s: jax.random.normal(k, shape, f32) * s
    D = D_MODEL
    return {
        "x": nrm(ks[0], (BATCH, SEQ, D), 1.0),
        "meta_tokens": nrm(ks[1], (N_META, D), 1.0),
        "attn_norm_g": 1.0 + nrm(ks[2], (DEPTH, D), 0.02),
        "w_in": nrm(ks[3], (DEPTH, D, IN_COLS), D ** -0.5),
        "forget_bias": FORGET_BIAS_MEAN + nrm(ks[4], (DEPTH, ATTN_HEADS), 0.5),
        "conv_w": nrm(ks[5], (DEPTH, CONV_KERNEL, CONV_WIDTH), CONV_KERNEL ** -0.5),
        "conv_b": nrm(ks[6], (DEPTH, CONV_WIDTH), 0.01),
        "conv_ln_g": 1.0 + nrm(ks[7], (DEPTH, CONV_WIDTH), 0.02),
        "conv_ln_b": nrm(ks[8], (DEPTH, CONV_WIDTH), 0.01),
        "w_out": nrm(ks[9], (DEPTH, MIX_WIDTH, D), MIX_WIDTH ** -0.5),
        "ffn_norm_g": 1.0 + nrm(ks[10], (DEPTH, D), 0.02),
        "w_router_group": nrm(ks[11], (DEPTH, D, N_GROUPS), D ** -0.5),
        "b_router_group": nrm(ks[12], (DEPTH, N_GROUPS), 0.01),
        "w_router_expert": nrm(ks[13], (DEPTH, D, N_EXPERTS), D ** -0.5),
        "b_router_expert": nrm(ks[14], (DEPTH, N_EXPERTS), 0.01),
        "w_expert_gate": nrm(ks[15], (DEPTH, N_EXPERTS, D, D_FF_EXPERT), D ** -0.5),
        "w_expert_up": nrm(ks[16], (DEPTH, N_EXPERTS, D, D_FF_EXPERT), D ** -0.5),
        "w_expert_down": nrm(ks[17], (DEPTH, N_EXPERTS, D_FF_EXPERT, D), D_FF_EXPERT ** -0.5),
        "final_norm_g": 1.0 + nrm(ks[18], (D,), 0.02),
    }


def reference(x, meta_tokens, attn_norm_g, w_in, forget_bias, conv_w, conv_b, conv_ln_g, conv_ln_b,
              w_out, ffn_norm_g, w_router_group, b_router_group, w_router_expert, b_router_expert,
              w_expert_gate, w_expert_up, w_expert_down, final_norm_g):
    B = x.shape[0]
    meta = jnp.broadcast_to(meta_tokens.astype(x.dtype)[None], (B, N_META, D_MODEL))
    h = jnp.concatenate([meta, x], axis=1)
    L = h.shape[1]
    splits = [ATTN_WIDTH, 2 * ATTN_WIDTH, 3 * ATTN_WIDTH, 3 * ATTN_WIDTH + ATTN_HEADS]
    for l in range(DEPTH):
        xn = rmsnorm(h, attn_norm_g[l])
        proj = jnp.einsum('bld,dc->blc', xn, w_in[l])
        q, k, v, f_logit, u = jnp.split(proj, splits, axis=-1)
        q = q.reshape(B, L, ATTN_HEADS, HEAD_DIM)
        k = k.reshape(B, L, ATTN_HEADS, HEAD_DIM)
        v = v.reshape(B, L, ATTN_HEADS, HEAD_DIM)
        logf = jax.nn.log_sigmoid(f_logit.astype(jnp.float32) + forget_bias[l].astype(jnp.float32))
        a = forgetting_attention(q, k, v, logf).reshape(B, L, ATTN_WIDTH)
        c = conformer_conv(u, conv_w[l], conv_b[l], conv_ln_g[l], conv_ln_b[l])
        mixed = jnp.concatenate([a, c], axis=-1)
        h = h + jnp.einsum('blm,md->bld', mixed, w_out[l])
        h = h + hierarchical_moe(rmsnorm(h, ffn_norm_g[l]), w_router_group[l], b_router_group[l],
                                 w_router_expert[l], b_router_expert[l], w_expert_gate[l],
                                 w_expert_up[l], w_expert_down[l])
    return rmsnorm(h[:, N_META:], final_norm_g)
```

```python
import functools

import jax
import jax.numpy as jnp
from jax import lax
from jax.experimental import pallas as pl
from jax.experimental.pallas import tpu as pltpu

F32 = jnp.float32
BF16 = jnp.bfloat16

N_META = 16
HEAD_DIM = 128
CONV_KERNEL = 31
N_GROUPS = 8
EXPERTS_PER_GROUP = 8
N_EXPERTS = N_GROUPS * EXPERTS_PER_GROUP
RMS_EPS = 1e-6
LN_EPS = 1e-5

LANES = 128
VMEM_LIMIT = 56 * 1024 * 1024
CONV_HALO = 32


def _params(*sem):
    return pltpu.CompilerParams(dimension_semantics=sem, vmem_limit_bytes=VMEM_LIMIT)


def _rmsnorm_kernel(x_ref, g_ref, o_ref):
    x = x_ref[...].astype(F32)
    y = x * lax.rsqrt(jnp.mean(x * x, axis=-1, keepdims=True) + RMS_EPS)
    o_ref[...] = (y * g_ref[...]).astype(o_ref.dtype)


def _rmsnorm(x, g, out_dtype, tm):
    m, d = x.shape
    return pl.pallas_call(
        _rmsnorm_kernel,
        grid=(m // tm,),
        in_specs=[pl.BlockSpec((tm, d), lambda i: (i, 0)), pl.BlockSpec((1, d), lambda i: (0, 0))],
        out_specs=pl.BlockSpec((tm, d), lambda i: (i, 0)),
        out_shape=jax.ShapeDtypeStruct((m, d), out_dtype),
        compiler_params=_params("parallel"),
        name="rmsnorm",
    )(x, g.reshape(1, d).astype(F32))


def _mm_kernel(a_ref, w_ref, o_ref):
    o_ref[...] = jnp.dot(a_ref[...], w_ref[...], preferred_element_type=F32).astype(o_ref.dtype)


def _matmul(a, w, out_dtype, tm, tn, name):
    m, k = a.shape
    n = w.shape[1]
    return pl.pallas_call(
        _mm_kernel,
        grid=(m // tm, n // tn),
        in_specs=[pl.BlockSpec((tm, k), lambda i, j: (i, 0)), pl.BlockSpec((k, tn), lambda i, j: (0, j))],
        out_specs=pl.BlockSpec((tm, tn), lambda i, j: (i, j)),
        out_shape=jax.ShapeDtypeStruct((m, n), out_dtype),
        compiler_params=_params("parallel", "parallel"),
        name=name,
    )(a, w)


def _log_sigmoid(z):
    return jnp.minimum(z, 0.0) - jnp.log1p(jnp.exp(-jnp.abs(z)))


def _cumsum_rows(l, tril):
    hi = l.astype(BF16)
    r1 = l - hi.astype(F32)
    mid = r1.astype(BF16)
    lo = (r1 - mid.astype(F32)).astype(BF16)
    dot = functools.partial(jnp.dot, preferred_element_type=F32)
    return dot(tril, hi) + dot(tril, mid) + dot(tril, lo)


def _tril(n):
    r = lax.broadcasted_iota(jnp.int32, (n, n), 0)
    c = lax.broadcasted_iota(jnp.int32, (n, n), 1)
    return jnp.where(r >= c, 1.0, 0.0).astype(BF16)


def _cumsum_kernel(fx_ref, fm_ref, b_ref, cx_ref, cm_ref, *, blk):
    bias = b_ref[...]
    cm = _cumsum_rows(_log_sigmoid(fm_ref[...] + bias), _tril(N_META))
    cm_ref[0] = cm
    carry = cm[N_META - 1:N_META, :]
    tril = _tril(blk)
    for j in range(fx_ref.shape[1] // blk):
        l = _log_sigmoid(fx_ref[0, j * blk:(j + 1) * blk, :] + bias)
        c = _cumsum_rows(l, tril) + carry
        cx_ref[0, j * blk:(j + 1) * blk, :] = c
        carry = c[blk - 1:blk, :]


def _forget_cumsum(fx, fm, bias):
    b, s, _ = fx.shape
    return pl.pallas_call(
        functools.partial(_cumsum_kernel, blk=256),
        grid=(b,),
        in_specs=[pl.BlockSpec((1, s, LANES), lambda i: (i, 0, 0)),
                  pl.BlockSpec((N_META, LANES), lambda i: (0, 0)),
                  pl.BlockSpec((1, LANES), lambda i: (0, 0))],
        out_specs=[pl.BlockSpec((1, s, LANES), lambda i: (i, 0, 0)),
                   pl.BlockSpec((1, N_META, LANES), lambda i: (i, 0, 0))],
        out_shape=[jax.ShapeDtypeStruct((b, s, LANES), F32), jax.ShapeDtypeStruct((b, N_META, LANES), F32)],
        compiler_params=_params("parallel"),
        name="forget_cumsum",
    )(fx, fm, bias)


def _attn_kernel(q_ref, k_ref, v_ref, km_ref, vm_ref, cx_ref, cm_ref, o_ref, *, tq):
    i = pl.program_id(2)
    q = q_ref[0]
    scale = HEAD_DIM ** -0.5

    def scores(kb, cb):
        s = lax.dot_general(q, kb, (((1,), (1,)), ((), ())), preferred_element_type=F32)
        return s * scale - cb

    s = scores(km_ref[...], cm_ref[0])
    col = lax.broadcasted_iota(jnp.int32, s.shape, 1)
    s = jnp.where(col < N_META, s, -jnp.inf)
    m = jnp.max(s, axis=-1, keepdims=True)
    p = jnp.exp(s - m)
    l = jnp.sum(p, axis=-1, keepdims=True)
    acc = jnp.dot(p.astype(BF16), vm_ref[...], preferred_element_type=F32)

    def update(s, vb, m, l, acc):
        m_new = jnp.maximum(m, jnp.max(s, axis=-1, keepdims=True))
        alpha = jnp.exp(m - m_new)
        p = jnp.exp(s - m_new)
        l = alpha * l + jnp.sum(p, axis=-1, keepdims=True)
        acc = alpha * acc + jnp.dot(p.astype(BF16), vb, preferred_element_type=F32)
        return m_new, l, acc

    def block(j):
        off = pl.multiple_of(j * tq, tq)
        return (scores(k_ref[0, pl.ds(off, tq), :], cx_ref[0, 0, pl.ds(j, 1), :]), v_ref[0, pl.ds(off, tq), :])

    def body(j, carry):
        s, vb = block(j)
        return update(s, vb, *carry)

    m, l, acc = lax.fori_loop(0, i, body, (m, l, acc))
    s, vb = block(i)
    row = lax.broadcasted_iota(jnp.int32, s.shape, 0)
    col = lax.broadcasted_iota(jnp.int32, s.shape, 1)
    s = jnp.where(row >= col, s, -jnp.inf)
    m, l, acc = update(s, vb, m, l, acc)
    o_ref[0] = (acc / l).astype(o_ref.dtype)


def _attention(qkv, kvm, cx, cm, n_heads, tq):
    b, s, _ = qkv.shape
    nk = s // tq
    blk = lambda f: pl.BlockSpec((1, tq, HEAD_DIM), f)
    return pl.pallas_call(
        functools.partial(_attn_kernel, tq=tq),
        grid=(b, n_heads, nk),
        in_specs=[blk(lambda bi, h, i: (bi, i, h)),
                  pl.BlockSpec((1, s, HEAD_DIM), lambda bi, h, i: (bi, 0, n_heads + h)),
                  pl.BlockSpec((1, s, HEAD_DIM), lambda bi, h, i: (bi, 0, 2 * n_heads + h)),
                  pl.BlockSpec((LANES, HEAD_DIM), lambda bi, h, i: (0, n_heads + h)),
                  pl.BlockSpec((LANES, HEAD_DIM), lambda bi, h, i: (0, 2 * n_heads + h)),
                  pl.BlockSpec((1, 1, nk, tq), lambda bi, h, i: (bi, h, 0, 0)),
                  pl.BlockSpec((1, 1, LANES), lambda bi, h, i: (h, 0, 0))],
        out_specs=blk(lambda bi, h, i: (bi, i, h)),
        out_shape=jax.ShapeDtypeStruct((b, s, n_heads * HEAD_DIM), BF16),
        compiler_params=_params("parallel", "parallel", "parallel"),
        name="fox_attention",
    )(qkv, qkv, qkv, kvm, kvm, cx, cm)


def _conv_kernel(ua_ref, ug_ref, ma_ref, mg_ref, w_ref, cb_ref, g_ref, b_ref, o_ref, hbuf, *, ts):
    t = pl.program_id(1)
    cw = hbuf.shape[1]

    @pl.when(t == 0)
    def _():
        hbuf[0:CONV_HALO - N_META, :] = jnp.zeros((CONV_HALO - N_META, cw), F32)
        hbuf[CONV_HALO - N_META:CONV_HALO, :] = ma_ref[...].astype(F32) * jax.nn.sigmoid(mg_ref[...].astype(F32))

    hbuf[CONV_HALO:CONV_HALO + ts, :] = ua_ref[0].astype(F32) * jax.nn.sigmoid(ug_ref[0].astype(F32))
    acc = jnp.zeros((ts, cw), F32) + cb_ref[...]
    base = CONV_HALO - (CONV_KERNEL - 1)
    for k in range(CONV_KERNEL):
        acc = acc + w_ref[k:k + 1, :] * hbuf[base + k:base + k + ts, :]
    mu = jnp.mean(acc, axis=-1, keepdims=True)
    xc = acc - mu
    var = jnp.mean(xc * xc, axis=-1, keepdims=True)
    y = xc * lax.rsqrt(var + LN_EPS) * g_ref[...] + b_ref[...]
    o_ref[0] = (y * jax.nn.sigmoid(y)).astype(o_ref.dtype)
    hbuf[0:CONV_HALO, :] = hbuf[ts:ts + CONV_HALO, :]


def _conv_branch(u, um, conv_w, conv_b, ln_g, ln_b, ts):
    b, s, c2 = u.shape
    c = c2 // 2
    row = lambda v: v.reshape(1, c).astype(F32)
    vec = pl.BlockSpec((1, c), lambda bi, t: (0, 0))
    return pl.pallas_call(
        functools.partial(_conv_kernel, ts=ts),
        grid=(b, s // ts),
        in_specs=[pl.BlockSpec((1, ts, c), lambda bi, t: (bi, t, 0)),
                  pl.BlockSpec((1, ts, c), lambda bi, t: (bi, t, 1)),
                  pl.BlockSpec((N_META, c), lambda bi, t: (0, 0)),
                  pl.BlockSpec((N_META, c), lambda bi, t: (0, 1)),
                  pl.BlockSpec((CONV_KERNEL, c), lambda bi, t: (0, 0)),
                  vec, vec, vec],
        out_specs=pl.BlockSpec((1, ts, c), lambda bi, t: (bi, t, 0)),
        out_shape=jax.ShapeDtypeStruct((b, s, c), BF16),
        scratch_shapes=[pltpu.VMEM((ts + CONV_HALO, c), F32)],
        compiler_params=_params("parallel", "arbitrary"),
        name="conformer_conv",
    )(u, u, um, um, conv_w.astype(F32), row(conv_b), row(ln_g), row(ln_b))


def _outproj_kernel(a_ref, c_ref, wa_ref, wc_ref, x_ref, o_ref):
    acc = jnp.dot(a_ref[...], wa_ref[...], preferred_element_type=F32)
    acc = acc + jnp.dot(c_ref[...], wc_ref[...], preferred_element_type=F32)
    o_ref[...] = x_ref[...] + acc


def _out_projection(a, c, w, x, tm, tn):
    m, ka = a.shape
    kc = c.shape[1]
    n = w.shape[1]
    assert ka == kc
    return pl.pallas_call(
        _outproj_kernel,
        grid=(m // tm, n // tn),
        in_specs=[pl.BlockSpec((tm, ka), lambda i, j: (i, 0)),
                  pl.BlockSpec((tm, kc), lambda i, j: (i, 0)),
                  pl.BlockSpec((ka, tn), lambda i, j: (0, j)),
                  pl.BlockSpec((kc, tn), lambda i, j: (1, j)),
                  pl.BlockSpec((tm, tn), lambda i, j: (i, j))],
        out_specs=pl.BlockSpec((tm, tn), lambda i, j: (i, j)),
        out_shape=jax.ShapeDtypeStruct((m, n), F32),
        compiler_params=_params("parallel", "parallel"),
        name="out_projection",
    )(a, c, w, w, x)


def _router_kernel(h_ref, g_ref, wr_ref, br_ref, xn_ref, info_ref):
    h = h_ref[...]
    xn = h * lax.rsqrt(jnp.mean(h * h, axis=-1, keepdims=True) + RMS_EPS) * g_ref[...]
    xn_ref[...] = xn
    logits = jnp.dot(xn, wr_ref[...], precision=lax.Precision.HIGHEST, preferred_element_type=F32) + br_ref[...]
    lane = lax.broadcasted_iota(jnp.int32, logits.shape, 1).astype(F32)
    rmax = lambda v: jnp.max(v, axis=-1, keepdims=True)
    rmin = lambda v: jnp.min(v, axis=-1, keepdims=True)
    rsum = lambda v: jnp.sum(v, axis=-1, keepdims=True)
    far = float(LANES)

    is_g = lane < N_GROUPS
    lg = jnp.where(is_g, logits, -jnp.inf)
    mg = rmax(lg)
    g_idx = rmin(jnp.where(lg == mg, lane, far))
    g_w = 1.0 / rsum(jnp.where(is_g, jnp.exp(lg - mg), 0.0))

    lo = N_GROUPS + EXPERTS_PER_GROUP * g_idx
    in_e = jnp.logical_and(lane >= lo, lane < lo + EXPERTS_PER_GROUP)
    le = jnp.where(in_e, logits, -jnp.inf)
    pe = jnp.where(in_e, jnp.exp(le - rmax(le)), 0.0)
    prob = jnp.where(in_e, pe / rsum(pe), -1.0)
    p1 = rmax(prob)
    i1 = rmin(jnp.where(prob == p1, lane, far))
    prob2 = jnp.where(lane == i1, -1.0, prob)
    p2 = rmax(prob2)
    i2 = rmin(jnp.where(prob2 == p2, lane, far))
    den = p1 + p2
    w1 = g_w * (p1 / den)
    w2 = g_w * (p2 / den)
    info = jnp.where(lane == 0, w1, jnp.where(lane == 1, w2, jnp.where(
        lane == 2, i1 - N_GROUPS, jnp.where(lane == 3, i2 - N_GROUPS, 0.0))))
    info_ref[...] = info


def _norm_router(h, g, wr, br, tm):
    m, d = h.shape
    return pl.pallas_call(
        _router_kernel,
        grid=(m // tm,),
        in_specs=[pl.BlockSpec((tm, d), lambda i: (i, 0)),
                  pl.BlockSpec((1, d), lambda i: (0, 0)),
                  pl.BlockSpec((d, LANES), lambda i: (0, 0)),
                  pl.BlockSpec((1, LANES), lambda i: (0, 0))],
        out_specs=[pl.BlockSpec((tm, d), lambda i: (i, 0)), pl.BlockSpec((tm, LANES), lambda i: (i, 0))],
        out_shape=[jax.ShapeDtypeStruct((m, d), F32), jax.ShapeDtypeStruct((m, LANES), F32)],
        compiler_params=_params("parallel"),
        name="norm_router",
    )(h, g.reshape(1, d).astype(F32), wr, br)


def _moe_kernel(te_ref, tn_ref, src_ref, dst_ref, xn_hbm, wg_ref, wu_ref, wd_ref, y_hbm,
                xbuf, xb, acc, gsem, ssem):
    t = pl.program_id(0)
    f = pl.program_id(1)
    nf = pl.num_programs(1)
    n = tn_ref[t]

    def gather_copy(r, tok):
        return pltpu.make_async_copy(xn_hbm.at[pl.ds(tok, 1), :], xbuf.at[pl.ds(r, 1), :], gsem)

    def scatter_copy(r, row):
        return pltpu.make_async_copy(acc.at[pl.ds(r, 1), :], y_hbm.at[pl.ds(row, 1), :], ssem)

    @pl.when(jnp.logical_and(t == 0, f == 0))
    def _():
        xbuf[...] = jnp.zeros(xbuf.shape, xbuf.dtype)

    @pl.when(jnp.logical_and(f == 0, n > 0))
    def _():
        def start(r, carry):
            gather_copy(r, src_ref[0, 0, r]).start()
            return carry
        lax.fori_loop(0, n, start, 0)

        def wait(r, carry):
            gather_copy(r, 0).wait()
            return carry
        lax.fori_loop(0, n, wait, 0)
        xb[...] = xbuf[...].astype(BF16)

    @pl.when(n > 0)
    def _():
        x = xb[...]
        g = jnp.dot(x, wg_ref[0].astype(BF16), preferred_element_type=F32)
        u = jnp.dot(x, wu_ref[0].astype(BF16), preferred_element_type=F32)
        hid = (g * jax.nn.sigmoid(g)) * u
        part = jnp.dot(hid.astype(BF16), wd_ref[0].astype(BF16), preferred_element_type=F32)

        @pl.when(f == 0)
        def _():
            acc[...] = part

        @pl.when(f > 0)
        def _():
            acc[...] = acc[...] + part

    @pl.when(jnp.logical_and(f == nf - 1, n > 0))
    def _():
        def start(r, carry):
            scatter_copy(r, dst_ref[0, 0, r]).start()
            return carry
        lax.fori_loop(0, n, start, 0)

        def wait(r, carry):
            scatter_copy(r, 0).wait()
            return carry
        lax.fori_loop(0, n, wait, 0)


def _moe(xn, tile_e, tile_n, src, dst, w_gate, w_up, w_down, tm, fc):
    t_rows, d = xn.shape
    n_tiles = tile_e.shape[0]
    ff = w_gate.shape[2]
    nf = ff // fc

    def fidx(f, tn, t):
        return jnp.where(tn[t] > 0, f, nf - 1)

    smem_blk = pl.BlockSpec((1, 1, tm), lambda t, f, te, tn: (t, 0, 0), memory_space=pltpu.SMEM)
    grid_spec = pltpu.PrefetchScalarGridSpec(
        num_scalar_prefetch=2,
        grid=(n_tiles, nf),
        in_specs=[smem_blk, smem_blk,
                  pl.BlockSpec(memory_space=pl.ANY),
                  pl.BlockSpec((1, d, fc), lambda t, f, te, tn: (te[t], 0, fidx(f, tn, t))),
                  pl.BlockSpec((1, d, fc), lambda t, f, te, tn: (te[t], 0, fidx(f, tn, t))),
                  pl.BlockSpec((1, fc, d), lambda t, f, te, tn: (te[t], fidx(f, tn, t), 0))],
        out_specs=pl.BlockSpec(memory_space=pl.ANY),
        scratch_shapes=[pltpu.VMEM((tm, d), F32), pltpu.VMEM((tm, d), BF16), pltpu.VMEM((tm, d), F32),
                        pltpu.SemaphoreType.DMA(()), pltpu.SemaphoreType.DMA(())],
    )
    return pl.pallas_call(
        _moe_kernel,
        grid_spec=grid_spec,
        out_shape=jax.ShapeDtypeStruct((2 * t_rows, d), F32),
        compiler_params=_params("arbitrary", "arbitrary"),
        name="sparse_moe",
    )(tile_e, tile_n, src, dst, xn, w_gate, w_up, w_down)


def _routing_tables(e_idx, tm, n_tiles):
    t_rows = e_idx.shape[0]
    flat_e = e_idx.reshape(-1)
    n_assign = flat_e.shape[0]
    onehot = (flat_e[:, None] == jnp.arange(N_EXPERTS, dtype=jnp.int32)[None, :]).astype(jnp.int32)
    csum = jnp.cumsum(onehot, axis=0)
    rank = jnp.take_along_axis(csum, flat_e[:, None], axis=1)[:, 0] - 1
    counts = csum[-1]
    tiles_per_e = (counts + tm - 1) // tm
    tile_end = jnp.cumsum(tiles_per_e)
    tile_start = tile_end - tiles_per_e
    total = tile_end[-1]
    pos = tile_start[flat_e] * tm + rank
    a = jnp.arange(n_assign, dtype=jnp.int32)
    tok = a // 2
    slot = a % 2
    src = jnp.zeros((n_tiles * tm,), jnp.int32).at[pos].set(tok)
    dst = jnp.zeros((n_tiles * tm,), jnp.int32).at[pos].set(slot * t_rows + tok)
    tid = jnp.arange(n_tiles, dtype=jnp.int32)
    te = jnp.searchsorted(tile_end, tid, side="right").astype(jnp.int32)
    te = jnp.minimum(te, N_EXPERTS - 1)
    active = tid < total
    tn = jnp.clip(counts[te] - (tid - tile_start[te]) * tm, 0, tm)
    tn = jnp.where(active, tn, 0).astype(jnp.int32)
    last_e = te[jnp.maximum(total - 1, 0)]
    te = jnp.where(active, te, last_e).astype(jnp.int32)
    return te, tn, src.reshape(n_tiles, 1, tm), dst.reshape(n_tiles, 1, tm)


def _final_kernel(h_ref, y0_ref, y1_ref, info_ref, g_ref, o_ref):
    w = info_ref[...]
    h = h_ref[...] + w[:, 0:1] * y0_ref[0] + w[:, 1:2] * y1_ref[0]
    y = h * lax.rsqrt(jnp.mean(h * h, axis=-1, keepdims=True) + RMS_EPS)
    o_ref[...] = y * g_ref[...]


def _combine_final(h, y, info, g, tm):
    m, d = h.shape
    return pl.pallas_call(
        _final_kernel,
        grid=(m // tm,),
        in_specs=[pl.BlockSpec((tm, d), lambda i: (i, 0)),
                  pl.BlockSpec((1, tm, d), lambda i: (0, i, 0)),
                  pl.BlockSpec((1, tm, d), lambda i: (1, i, 0)),
                  pl.BlockSpec((tm, LANES), lambda i: (i, 0)),
                  pl.BlockSpec((1, d), lambda i: (0, 0))],
        out_specs=pl.BlockSpec((tm, d), lambda i: (i, 0)),
        out_shape=jax.ShapeDtypeStruct((m, d), F32),
        compiler_params=_params("parallel"),
        name="combine_final",
    )(h, y, y, info, g.reshape(1, d).astype(F32))


def kernel(x, meta_tokens, attn_norm_g, w_in, forget_bias, conv_w, conv_b, conv_ln_g, conv_ln_b, w_out,
           ffn_norm_g, w_router_group, b_router_group, w_router_expert, b_router_expert,
           w_expert_gate, w_expert_up, w_expert_down, final_norm_g):
    b, s, d = x.shape
    depth = w_in.shape[0]
    assert depth == 1, "one layer: the meta rows are not carried past the mixer"
    n_heads = forget_bias.shape[1]
    aw = n_heads * HEAD_DIM
    cw = conv_w.shape[2]
    t_rows = b * s
    x2 = x.reshape(t_rows, d)

    w_in0 = w_in[0]
    w_qkv = w_in0[:, :3 * aw].astype(BF16)
    w_f = jnp.zeros((d, LANES), BF16).at[:, :n_heads].set(w_in0[:, 3 * aw:3 * aw + n_heads].astype(BF16))
    w_u = w_in0[:, 3 * aw + n_heads:].astype(BF16)
    bias_f = jnp.zeros((1, LANES), F32).at[0, :n_heads].set(forget_bias[0].astype(F32))

    xn = _rmsnorm(x2, attn_norm_g[0], BF16, tm=512)
    mn = _rmsnorm(meta_tokens.astype(x.dtype), attn_norm_g[0], BF16, tm=N_META)

    qkv = _matmul(xn, w_qkv, BF16, 1024, 1024, "in_proj_qkv")
    fx = _matmul(xn, w_f, F32, 1024, LANES, "in_proj_gate")
    u = _matmul(xn, w_u, BF16, 1024, 1024, "in_proj_glu")
    qkv_m = _matmul(mn, w_qkv, BF16, N_META, 1024, "meta_proj_qkv")
    f_m = _matmul(mn, w_f, F32, N_META, LANES, "meta_proj_gate")
    u_m = _matmul(mn, w_u, BF16, N_META, 1024, "meta_proj_glu")

    tq = 512
    cx, cm = _forget_cumsum(fx.reshape(b, s, LANES), f_m, bias_f)
    cx = jnp.transpose(cx[:, :, :n_heads], (0, 2, 1)).reshape(b, n_heads, s // tq, tq)
    cm = jnp.zeros((n_heads, 1, LANES), F32).at[:, 0, :N_META].set(jnp.transpose(cm[0, :, :n_heads]))
    kvm = jnp.zeros((LANES, 3 * aw), BF16).at[:N_META].set(qkv_m)
    attn = _attention(qkv.reshape(b, s, 3 * aw), kvm, cx, cm, n_heads, tq)

    conv = _conv_branch(u.reshape(b, s, 2 * cw), u_m, conv_w[0], conv_b[0], conv_ln_g[0], conv_ln_b[0], ts=256)

    h1 = _out_projection(attn.reshape(t_rows, aw), conv.reshape(t_rows, cw), w_out[0].astype(BF16), x2, 1024, 512)

    n_r = N_GROUPS + N_EXPERTS
    wr = jnp.zeros((d, LANES), F32).at[:, :N_GROUPS].set(w_router_group[0]).at[:, N_GROUPS:n_r].set(w_router_expert[0])
    br = jnp.zeros((1, LANES), F32).at[0, :N_GROUPS].set(b_router_group[0]).at[0, N_GROUPS:n_r].set(b_router_expert[0])
    xn2, info = _norm_router(h1, ffn_norm_g[0], wr, br, tm=256)

    tm_moe = 512
    n_tiles = (2 * t_rows) // tm_moe + N_EXPERTS
    te, tn, src, dst = _routing_tables(info[:, 2:4].astype(jnp.int32), tm_moe, n_tiles)
    y = _moe(xn2, te, tn, src, dst, w_expert_gate[0], w_expert_up[0], w_expert_down[0], tm_moe, fc=256)

    out = _combine_final(h1, y.reshape(2, t_rows, d), info, final_norm_g, tm=256)
    return out.reshape(b, s, d)
```

```python
import functools

import jax
import jax.numpy as jnp
from jax import lax
from jax.experimental import pallas as pl
from jax.experimental.pallas import tpu as pltpu

F32 = jnp.float32
BF16 = jnp.bfloat16
U32 = jnp.uint32

N_META = 16
HEAD_DIM = 128
CONV_KERNEL = 31
N_GROUPS = 8
EXPERTS_PER_GROUP = 8
N_EXPERTS = N_GROUPS * EXPERTS_PER_GROUP
RMS_EPS = 1e-6
LN_EPS = 1e-5

LANES = 128
SUBLANES = 8
VMEM_LIMIT = 56 * 1024 * 1024
CONV_HALO = 32
CONV_ROWS = 64


def _params(*sem):
    return pltpu.CompilerParams(dimension_semantics=sem, vmem_limit_bytes=VMEM_LIMIT)


def _pack_bf16_pair(lo, hi):
    lo_bits = lax.bitcast_convert_type(lo.astype(BF16).astype(F32), U32)
    hi_bits = lax.bitcast_convert_type(hi.astype(BF16).astype(F32), U32)
    return (lo_bits >> 16) | hi_bits


def _unpack_bf16_pair(p):
    lo = lax.bitcast_convert_type(p << 16, F32)
    hi = lax.bitcast_convert_type(p & jnp.uint32(0xFFFF0000), F32)
    return lo, hi


def _rmsnorm_kernel(x_ref, g_ref, o_ref):
    x = x_ref[...].astype(F32)
    y = x * lax.rsqrt(jnp.mean(x * x, axis=-1, keepdims=True) + RMS_EPS)
    o_ref[...] = (y * g_ref[...]).astype(o_ref.dtype)


def _rmsnorm(x, g, out_dtype, tm):
    m, d = x.shape
    return pl.pallas_call(
        _rmsnorm_kernel,
        grid=(m // tm,),
        in_specs=[pl.BlockSpec((tm, d), lambda i: (i, 0)), pl.BlockSpec((1, d), lambda i: (0, 0))],
        out_specs=pl.BlockSpec((tm, d), lambda i: (i, 0)),
        out_shape=jax.ShapeDtypeStruct((m, d), out_dtype),
        compiler_params=_params("parallel"),
        name="rmsnorm",
    )(x, g.reshape(1, d).astype(F32))


def _proj_kernel(a_ref, am_ref, w_ref, o_ref, om_ref, wb_ref):
    i = pl.program_id(1)

    @pl.when(i == 0)
    def _():
        wb_ref[...] = w_ref[0].astype(BF16)
        om_ref[...] = jnp.dot(am_ref[...], wb_ref[...], preferred_element_type=F32).astype(om_ref.dtype)

    o_ref[...] = jnp.dot(a_ref[...], wb_ref[...], preferred_element_type=F32).astype(o_ref.dtype)


def _projection(a, am, w, col0, n_cols, out_dtype, tm, tn, name):
    m, k = a.shape
    mm = am.shape[0]
    assert col0 % tn == 0 and n_cols % tn == 0 and m % tm == 0
    jb = col0 // tn
    return pl.pallas_call(
        _proj_kernel,
        grid=(n_cols // tn, m // tm),
        in_specs=[pl.BlockSpec((tm, k), lambda j, i: (i, 0)),
                  pl.BlockSpec((mm, k), lambda j, i: (0, 0)),
                  pl.BlockSpec((1, k, tn), lambda j, i: (0, 0, jb + j))],
        out_specs=[pl.BlockSpec((tm, tn), lambda j, i: (i, j)), pl.BlockSpec((mm, tn), lambda j, i: (0, j))],
        out_shape=[jax.ShapeDtypeStruct((m, n_cols), out_dtype), jax.ShapeDtypeStruct((mm, n_cols), out_dtype)],
        scratch_shapes=[pltpu.VMEM((k, tn), BF16)],
        compiler_params=_params("parallel", "arbitrary"),
        name=name,
    )(a, am, w)


def _log_sigmoid(z):
    return jnp.minimum(z, 0.0) - jnp.log1p(jnp.exp(-jnp.abs(z)))


def _cumsum_rows(l, tril):
    hi = l.astype(BF16)
    r1 = l - hi.astype(F32)
    mid = r1.astype(BF16)
    lo = (r1 - mid.astype(F32)).astype(BF16)
    dot = functools.partial(jnp.dot, preferred_element_type=F32)
    return dot(tril, hi) + dot(tril, mid) + dot(tril, lo)


def _tril(n):
    r = lax.broadcasted_iota(jnp.int32, (n, n), 0)
    c = lax.broadcasted_iota(jnp.int32, (n, n), 1)
    return jnp.where(r >= c, 1.0, 0.0).astype(BF16)


def _cumsum_kernel(fx_ref, fm_ref, b_ref, cx_ref, cm_ref, *, blk):
    bias = b_ref[...]
    cm = _cumsum_rows(_log_sigmoid(fm_ref[...] + bias), _tril(N_META))
    cm_ref[0] = cm
    carry = cm[N_META - 1:N_META, :]
    tril = _tril(blk)
    for j in range(fx_ref.shape[1] // blk):
        l = _log_sigmoid(fx_ref[0, j * blk:(j + 1) * blk, :] + bias)
        c = _cumsum_rows(l, tril) + carry
        cx_ref[0, j * blk:(j + 1) * blk, :] = c
        carry = c[blk - 1:blk, :]


def _forget_cumsum(fx, fm, bias):
    b, s, _ = fx.shape
    return pl.pallas_call(
        functools.partial(_cumsum_kernel, blk=256),
        grid=(b,),
        in_specs=[pl.BlockSpec((1, s, LANES), lambda i: (i, 0, 0)),
                  pl.BlockSpec((N_META, LANES), lambda i: (0, 0)),
                  pl.BlockSpec((1, LANES), lambda i: (0, 0))],
        out_specs=[pl.BlockSpec((1, s, LANES), lambda i: (i, 0, 0)),
                   pl.BlockSpec((1, N_META, LANES), lambda i: (i, 0, 0))],
        out_shape=[jax.ShapeDtypeStruct((b, s, LANES), F32), jax.ShapeDtypeStruct((b, N_META, LANES), F32)],
        compiler_params=_params("parallel"),
        name="forget_cumsum",
    )(fx, fm, bias)


def _attn_kernel(q_ref, k_ref, v_ref, km_ref, vm_ref, cx_ref, cm_ref, o_ref, *, tq):
    i = pl.program_id(2)
    q = q_ref[0]
    scale = HEAD_DIM ** -0.5

    def scores(kb, cb):
        s = lax.dot_general(q, kb, (((1,), (1,)), ((), ())), preferred_element_type=F32)
        return s * scale - cb

    s = scores(km_ref[...], cm_ref[0])
    col = lax.broadcasted_iota(jnp.int32, s.shape, 1)
    s = jnp.where(col < N_META, s, -jnp.inf)
    m = jnp.max(s, axis=-1, keepdims=True)
    p = jnp.exp(s - m)
    l = jnp.sum(p, axis=-1, keepdims=True)
    acc = jnp.dot(p.astype(BF16), vm_ref[...], preferred_element_type=F32)

    def update(s, vb, m, l, acc):
        m_new = jnp.maximum(m, jnp.max(s, axis=-1, keepdims=True))
        alpha = jnp.exp(m - m_new)
        p = jnp.exp(s - m_new)
        l = alpha * l + jnp.sum(p, axis=-1, keepdims=True)
        acc = alpha * acc + jnp.dot(p.astype(BF16), vb, preferred_element_type=F32)
        return m_new, l, acc

    def block(j):
        off = pl.multiple_of(j * tq, tq)
        return (scores(k_ref[0, pl.ds(off, tq), :], cx_ref[0, 0, pl.ds(j, 1), :]), v_ref[0, pl.ds(off, tq), :])

    def body(j, carry):
        s, vb = block(j)
        return update(s, vb, *carry)

    m, l, acc = lax.fori_loop(0, i, body, (m, l, acc))
    s, vb = block(i)
    row = lax.broadcasted_iota(jnp.int32, s.shape, 0)
    col = lax.broadcasted_iota(jnp.int32, s.shape, 1)
    s = jnp.where(row >= col, s, -jnp.inf)
    m, l, acc = update(s, vb, m, l, acc)
    o_ref[0] = (acc / l).astype(o_ref.dtype)


def _attention(qkv, kvm, cx, cm, n_heads, tq):
    b, s, _ = qkv.shape
    nk = s // tq
    blk = lambda f: pl.BlockSpec((1, tq, HEAD_DIM), f)
    return pl.pallas_call(
        functools.partial(_attn_kernel, tq=tq),
        grid=(b, n_heads, nk),
        in_specs=[blk(lambda bi, h, i: (bi, i, h)),
                  pl.BlockSpec((1, s, HEAD_DIM), lambda bi, h, i: (bi, 0, n_heads + h)),
                  pl.BlockSpec((1, s, HEAD_DIM), lambda bi, h, i: (bi, 0, 2 * n_heads + h)),
                  pl.BlockSpec((LANES, HEAD_DIM), lambda bi, h, i: (0, n_heads + h)),
                  pl.BlockSpec((LANES, HEAD_DIM), lambda bi, h, i: (0, 2 * n_heads + h)),
                  pl.BlockSpec((1, 1, nk, tq), lambda bi, h, i: (bi, h, 0, 0)),
                  pl.BlockSpec((1, 1, LANES), lambda bi, h, i: (h, 0, 0))],
        out_specs=blk(lambda bi, h, i: (bi, i, h)),
        out_shape=jax.ShapeDtypeStruct((b, s, n_heads * HEAD_DIM), BF16),
        compiler_params=_params("parallel", "parallel", "parallel"),
        name="fox_attention",
    )(qkv, qkv, qkv, kvm, kvm, cx, cm)


def _conv_kernel(ua_ref, ug_ref, ma_ref, mg_ref, w_ref, cb_ref, g_ref, b_ref, o_ref, hb, hs, cbuf, *, ts):
    t = pl.program_id(1)
    nch = hb.shape[0]
    off = CONV_HALO - (CONV_KERNEL - 1)
    span = ts + CONV_HALO - SUBLANES
    lanes = lambda j: slice(j * LANES, (j + 1) * LANES)

    @pl.when(t == 0)
    def _():
        for j in range(nch):
            hb[j, 0:CONV_HALO - N_META, :] = jnp.zeros((CONV_HALO - N_META, LANES), F32)
            hb[j, CONV_HALO - N_META:CONV_HALO, :] = (
                ma_ref[:, lanes(j)].astype(F32) * jax.nn.sigmoid(mg_ref[:, lanes(j)].astype(F32)))

    for j in range(nch):
        hb[j, CONV_HALO:CONV_HALO + ts, :] = (
            ua_ref[0, :, lanes(j)].astype(F32) * jax.nn.sigmoid(ug_ref[0, :, lanes(j)].astype(F32)))

    def lane_chunk(j, carry):
        for s in range(1, SUBLANES):
            hs[s - 1] = hb[j, pl.ds(s, span), :]

        def row_chunk(i, c2):
            r0 = pl.multiple_of(i * CONV_ROWS, CONV_ROWS)
            tile_rows = lambda v: jnp.concatenate([v] * (CONV_ROWS // SUBLANES), axis=0)
            acc = tile_rows(cb_ref[j])
            for k in range(CONV_KERNEL):
                a, s = divmod(off + k, SUBLANES)
                rows = pl.ds(r0 + SUBLANES * a, CONV_ROWS)
                src = hb[j, rows, :] if s == 0 else hs[s - 1, rows, :]
                acc = acc + tile_rows(w_ref[j, k]) * src
            cbuf[j, pl.ds(r0, CONV_ROWS), :] = acc
            return c2

        lax.fori_loop(0, ts // CONV_ROWS, row_chunk, 0)
        return carry

    lax.fori_loop(0, nch, lane_chunk, 0)

    for j in range(nch):
        hb[j, 0:CONV_HALO, :] = hb[j, ts:ts + CONV_HALO, :]

    c = cbuf[...]
    n_ch = nch * LANES
    mu = jnp.sum(jnp.sum(c, axis=0), axis=-1, keepdims=True) / n_ch
    xc = c - mu[None]
    var = jnp.sum(jnp.sum(xc * xc, axis=0), axis=-1, keepdims=True) / n_ch
    y = xc * lax.rsqrt(var + LN_EPS)[None] * g_ref[...] + b_ref[...]
    y = y * jax.nn.sigmoid(y)
    for j in range(nch):
        o_ref[0, :, lanes(j)] = y[j].astype(o_ref.dtype)


def _conv_branch(u, um, conv_w, conv_b, ln_g, ln_b, ts):
    b, s, c2 = u.shape
    c = c2 // 2
    nch = c // LANES
    chunked = lambda v: jnp.transpose(v.astype(F32).reshape(-1, nch, LANES), (1, 0, 2))
    w_rep = jnp.broadcast_to(chunked(conv_w)[:, :, None, :], (nch, CONV_KERNEL, SUBLANES, LANES))
    cb_rep = jnp.broadcast_to(chunked(conv_b), (nch, SUBLANES, LANES))
    full = lambda shape: pl.BlockSpec(shape, lambda bi, t: (0,) * len(shape))
    return pl.pallas_call(
        functools.partial(_conv_kernel, ts=ts),
        grid=(b, s // ts),
        in_specs=[pl.BlockSpec((1, ts, c), lambda bi, t: (bi, t, 0)),
                  pl.BlockSpec((1, ts, c), lambda bi, t: (bi, t, 1)),
                  pl.BlockSpec((N_META, c), lambda bi, t: (0, 0)),
                  pl.BlockSpec((N_META, c), lambda bi, t: (0, 1)),
                  full((nch, CONV_KERNEL, SUBLANES, LANES)),
                  full((nch, SUBLANES, LANES)),
                  full((nch, 1, LANES)),
                  full((nch, 1, LANES))],
        out_specs=pl.BlockSpec((1, ts, c), lambda bi, t: (bi, t, 0)),
        out_shape=jax.ShapeDtypeStruct((b, s, c), BF16),
        scratch_shapes=[pltpu.VMEM((nch, ts + CONV_HALO, LANES), F32),
                        pltpu.VMEM((SUBLANES - 1, ts + CONV_HALO - SUBLANES, LANES), F32),
                        pltpu.VMEM((nch, ts, LANES), F32)],
        compiler_params=_params("parallel", "arbitrary"),
        name="conformer_conv",
    )(u, u, um, um, w_rep, cb_rep, chunked(ln_g), chunked(ln_b))


def _outproj_kernel(a_ref, c_ref, wa_ref, wc_ref, x_ref, o_ref, wab_ref, wcb_ref):
    @pl.when(pl.program_id(1) == 0)
    def _():
        wab_ref[...] = wa_ref[0].astype(BF16)
        wcb_ref[...] = wc_ref[0].astype(BF16)

    acc = jnp.dot(a_ref[...], wab_ref[...], preferred_element_type=F32)
    acc = acc + jnp.dot(c_ref[...], wcb_ref[...], preferred_element_type=F32)
    o_ref[...] = x_ref[...] + acc


def _out_projection(a, c, w, x, tm, tn):
    m, ka = a.shape
    kc = c.shape[1]
    n = w.shape[2]
    assert ka == kc
    return pl.pallas_call(
        _outproj_kernel,
        grid=(n // tn, m // tm),
        in_specs=[pl.BlockSpec((tm, ka), lambda j, i: (i, 0)),
                  pl.BlockSpec((tm, kc), lambda j, i: (i, 0)),
                  pl.BlockSpec((1, ka, tn), lambda j, i: (0, 0, j)),
                  pl.BlockSpec((1, kc, tn), lambda j, i: (0, 1, j)),
                  pl.BlockSpec((tm, tn), lambda j, i: (i, j))],
        out_specs=pl.BlockSpec((tm, tn), lambda j, i: (i, j)),
        out_shape=jax.ShapeDtypeStruct((m, n), F32),
        scratch_shapes=[pltpu.VMEM((ka, tn), BF16), pltpu.VMEM((kc, tn), BF16)],
        compiler_params=_params("parallel", "arbitrary"),
        name="out_projection",
    )(a, c, w, w, x)


def _router_kernel(h_ref, g_ref, wr_ref, br_ref, xp_ref, info_ref):
    h = h_ref[...]
    xn = h * lax.rsqrt(jnp.mean(h * h, axis=-1, keepdims=True) + RMS_EPS) * g_ref[...]
    half = xn.shape[1] // 2
    xp_ref[...] = _pack_bf16_pair(xn[:, :half], xn[:, half:])
    logits = jnp.dot(xn, wr_ref[...], precision=lax.Precision.HIGHEST, preferred_element_type=F32) + br_ref[...]
    lane = lax.broadcasted_iota(jnp.int32, logits.shape, 1).astype(F32)
    rmax = lambda v: jnp.max(v, axis=-1, keepdims=True)
    rmin = lambda v: jnp.min(v, axis=-1, keepdims=True)
    rsum = lambda v: jnp.sum(v, axis=-1, keepdims=True)
    far = float(LANES)

    is_g = lane < N_GROUPS
    lg = jnp.where(is_g, logits, -jnp.inf)
    mg = rmax(lg)
    g_idx = rmin(jnp.where(lg == mg, lane, far))
    g_w = 1.0 / rsum(jnp.where(is_g, jnp.exp(lg - mg), 0.0))

    lo = N_GROUPS + EXPERTS_PER_GROUP * g_idx
    in_e = jnp.logical_and(lane >= lo, lane < lo + EXPERTS_PER_GROUP)
    le = jnp.where(in_e, logits, -jnp.inf)
    pe = jnp.where(in_e, jnp.exp(le - rmax(le)), 0.0)
    prob = jnp.where(in_e, pe / rsum(pe), -1.0)
    p1 = rmax(prob)
    i1 = rmin(jnp.where(prob == p1, lane, far))
    prob2 = jnp.where(lane == i1, -1.0, prob)
    p2 = rmax(prob2)
    i2 = rmin(jnp.where(prob2 == p2, lane, far))
    den = p1 + p2
    w1 = g_w * (p1 / den)
    w2 = g_w * (p2 / den)
    info = jnp.where(lane == 0, w1, jnp.where(lane == 1, w2, jnp.where(
        lane == 2, i1 - N_GROUPS, jnp.where(lane == 3, i2 - N_GROUPS, 0.0))))
    info_ref[...] = info


def _norm_router(h, g, wr, br, tm):
    m, d = h.shape
    return pl.pallas_call(
        _router_kernel,
        grid=(m // tm,),
        in_specs=[pl.BlockSpec((tm, d), lambda i: (i, 0)),
                  pl.BlockSpec((1, d), lambda i: (0, 0)),
                  pl.BlockSpec((d, LANES), lambda i: (0, 0)),
                  pl.BlockSpec((1, LANES), lambda i: (0, 0))],
        out_specs=[pl.BlockSpec((tm, d // 2), lambda i: (i, 0)), pl.BlockSpec((tm, LANES), lambda i: (i, 0))],
        out_shape=[jax.ShapeDtypeStruct((m, d // 2), U32), jax.ShapeDtypeStruct((m, LANES), F32)],
        compiler_params=_params("parallel"),
        name="norm_router",
    )(h, g.reshape(1, d).astype(F32), wr, br)


def _moe_kernel(te_ref, tn_ref, tb_ref, dst_ref, xp_hbm, wg_ref, wu_ref, wd_ref, y_hbm,
                xq, xb, acc, ystage, pend, gsem, ssem, *, t_rows):
    t = pl.program_id(0)
    f = pl.program_id(1)
    n_tiles = pl.num_programs(0)
    nf = pl.num_programs(1)
    n = tn_ref[t]
    slot = lax.rem(t, 2)
    half = xq.shape[2]

    def gather_copy(buf, r, tok):
        return pltpu.make_async_copy(xp_hbm.at[pl.ds(tok, 1), :], xq.at[buf, pl.ds(r, 1), :], gsem.at[buf])

    def scatter_copy(r, row):
        return pltpu.make_async_copy(ystage.at[pl.ds(r, 1), :], y_hbm.at[pl.ds(row, 1), :], ssem)

    def start_gather(tile, buf):
        base = tb_ref[tile]

        def body(r, carry):
            gather_copy(buf, r, lax.rem(dst_ref[base + r], t_rows)).start()
            return carry
        lax.fori_loop(0, tn_ref[tile], body, 0)

    def wait_scatter():
        def body(r, carry):
            scatter_copy(r, 0).wait()
            return carry
        lax.fori_loop(0, pend[0], body, 0)
        pend[0] = 0

    @pl.when(jnp.logical_and(t == 0, f == 0))
    def _():
        xq[...] = jnp.zeros(xq.shape, xq.dtype)
        pend[0] = 0
        start_gather(0, 0)

    @pl.when(jnp.logical_and(f == 0, n > 0))
    def _():
        def wait(r, carry):
            gather_copy(slot, r, 0).wait()
            return carry
        lax.fori_loop(0, n, wait, 0)
        lo, hi = _unpack_bf16_pair(xq[slot])
        xb[:, :half] = lo.astype(BF16)
        xb[:, half:] = hi.astype(BF16)

    @pl.when(jnp.logical_and(f == 0, t + 1 < n_tiles))
    def _():
        start_gather(t + 1, 1 - slot)

    @pl.when(n > 0)
    def _():
        x = xb[...]
        g = jnp.dot(x, wg_ref[0].astype(BF16), preferred_element_type=F32)
        u = jnp.dot(x, wu_ref[0].astype(BF16), preferred_element_type=F32)
        hid = (g * jax.nn.sigmoid(g)) * u
        part = jnp.dot(hid.astype(BF16), wd_ref[0].astype(BF16), preferred_element_type=F32)

        @pl.when(f == 0)
        def _():
            acc[...] = part

        @pl.when(f > 0)
        def _():
            acc[...] = acc[...] + part

    @pl.when(jnp.logical_and(f == nf - 1, n > 0))
    def _():
        wait_scatter()
        ystage[...] = _pack_bf16_pair(acc[:, :half], acc[:, half:])
        base = tb_ref[t]

        def start(r, carry):
            scatter_copy(r, dst_ref[base + r]).start()
            return carry
        lax.fori_loop(0, n, start, 0)
        pend[0] = n

    @pl.when(jnp.logical_and(f == nf - 1, t == n_tiles - 1))
    def _():
        wait_scatter()


def _moe(xp, tile_e, tile_n, tile_base, dst, w_gate, w_up, w_down, tm, fc):
    t_rows, half = xp.shape
    d = 2 * half
    n_tiles = tile_e.shape[0]
    ff = w_gate.shape[3]
    nf = ff // fc

    def fidx(f, tn, t):
        return jnp.where(tn[t] > 0, f, nf - 1)

    grid_spec = pltpu.PrefetchScalarGridSpec(
        num_scalar_prefetch=4,
        grid=(n_tiles, nf),
        in_specs=[pl.BlockSpec(memory_space=pl.ANY),
                  pl.BlockSpec((1, 1, d, fc), lambda t, f, te, tn, tb, ds: (0, te[t], 0, fidx(f, tn, t))),
                  pl.BlockSpec((1, 1, d, fc), lambda t, f, te, tn, tb, ds: (0, te[t], 0, fidx(f, tn, t))),
                  pl.BlockSpec((1, 1, fc, d), lambda t, f, te, tn, tb, ds: (0, te[t], fidx(f, tn, t), 0))],
        out_specs=pl.BlockSpec(memory_space=pl.ANY),
        scratch_shapes=[pltpu.VMEM((2, tm, half), U32), pltpu.VMEM((tm, d), BF16), pltpu.VMEM((tm, d), F32),
                        pltpu.VMEM((tm, half), U32), pltpu.SMEM((1,), jnp.int32),
                        pltpu.SemaphoreType.DMA((2,)), pltpu.SemaphoreType.DMA(())],
    )
    return pl.pallas_call(
        functools.partial(_moe_kernel_4d, t_rows=t_rows),
        grid_spec=grid_spec,
        out_shape=jax.ShapeDtypeStruct((2 * t_rows, half), U32),
        compiler_params=_params("arbitrary", "arbitrary"),
        name="sparse_moe",
    )(tile_e, tile_n, tile_base, dst, xp, w_gate, w_up, w_down)


def _moe_kernel_4d(te_ref, tn_ref, tb_ref, dst_ref, xp_hbm, wg_ref, wu_ref, wd_ref, *rest, t_rows):
    _moe_kernel(te_ref, tn_ref, tb_ref, dst_ref, xp_hbm, wg_ref.at[0], wu_ref.at[0], wd_ref.at[0], *rest,
                t_rows=t_rows)


def _routing_tables(e_idx, tm, n_tiles):
    t_rows = e_idx.shape[0]
    flat_e = e_idx.reshape(-1)
    n_assign = flat_e.shape[0]
    onehot = (flat_e[:, None] == jnp.arange(N_EXPERTS, dtype=jnp.int32)[None, :]).astype(jnp.int32)
    csum = jnp.cumsum(onehot, axis=0)
    rank = jnp.sum(csum * onehot, axis=1) - 1
    counts = csum[-1]
    row_end = jnp.cumsum(counts)
    row_start = row_end - counts
    tiles_per_e = (counts + tm - 1) // tm
    tile_end = jnp.cumsum(tiles_per_e)
    tile_start = tile_end - tiles_per_e
    total = tile_end[-1]
    pos = jnp.sum(onehot * row_start[None, :], axis=1) + rank
    a = jnp.arange(n_assign, dtype=jnp.int32)
    dst = jnp.zeros((n_assign,), jnp.int32).at[pos].set((a % 2) * t_rows + a // 2, unique_indices=True)
    tid = jnp.arange(n_tiles, dtype=jnp.int32)
    te = jnp.sum((tid[:, None] >= tile_end[None, :]).astype(jnp.int32), axis=1)
    te = jnp.minimum(te, N_EXPERTS - 1)
    active = tid < total
    first = (tid - tile_start[te]) * tm
    tn = jnp.where(active, jnp.clip(counts[te] - first, 0, tm), 0).astype(jnp.int32)
    tb = jnp.where(active, row_start[te] + first, 0).astype(jnp.int32)
    last_e = te[jnp.maximum(total - 1, 0)]
    te = jnp.where(active, te, last_e).astype(jnp.int32)
    return te, tn, tb, dst


def _final_kernel(h_ref, y0_ref, y1_ref, info_ref, g_ref, o_ref):
    w = info_ref[...]
    w0 = w[:, 0:1]
    w1 = w[:, 1:2]
    half = y0_ref.shape[2]
    y0_lo, y0_hi = _unpack_bf16_pair(y0_ref[0])
    y1_lo, y1_hi = _unpack_bf16_pair(y1_ref[0])
    h_lo = h_ref[:, :half] + w0 * y0_lo + w1 * y1_lo
    h_hi = h_ref[:, half:] + w0 * y0_hi + w1 * y1_hi
    ms = (jnp.sum(h_lo * h_lo, axis=-1, keepdims=True) + jnp.sum(h_hi * h_hi, axis=-1, keepdims=True)) / (2 * half)
    r = lax.rsqrt(ms + RMS_EPS)
    o_ref[:, :half] = h_lo * r * g_ref[:, :half]
    o_ref[:, half:] = h_hi * r * g_ref[:, half:]


def _combine_final(h, y, info, g, tm):
    m, d = h.shape
    return pl.pallas_call(
        _final_kernel,
        grid=(m // tm,),
        in_specs=[pl.BlockSpec((tm, d), lambda i: (i, 0)),
                  pl.BlockSpec((1, tm, d // 2), lambda i: (0, i, 0)),
                  pl.BlockSpec((1, tm, d // 2), lambda i: (1, i, 0)),
                  pl.BlockSpec((tm, LANES), lambda i: (i, 0)),
                  pl.BlockSpec((1, d), lambda i: (0, 0))],
        out_specs=pl.BlockSpec((tm, d), lambda i: (i, 0)),
        out_shape=jax.ShapeDtypeStruct((m, d), F32),
        compiler_params=_params("parallel"),
        name="combine_final",
    )(h, y, y, info, g.reshape(1, d).astype(F32))


def kernel(x, meta_tokens, attn_norm_g, w_in, forget_bias, conv_w, conv_b, conv_ln_g, conv_ln_b, w_out,
           ffn_norm_g, w_router_group, b_router_group, w_router_expert, b_router_expert,
           w_expert_gate, w_expert_up, w_expert_down, final_norm_g):
    b, s, d = x.shape
    depth = w_in.shape[0]
    assert depth == 1, "one layer: the meta rows are not carried past the mixer"
    n_heads = forget_bias.shape[1]
    aw = n_heads * HEAD_DIM
    cw = conv_w.shape[2]
    t_rows = b * s
    x2 = x.reshape(t_rows, d)

    xn = _rmsnorm(x2, attn_norm_g[0], BF16, tm=512)
    mn = _rmsnorm(meta_tokens.astype(x.dtype), attn_norm_g[0], BF16, tm=N_META)

    qkv, qkv_m = _projection(xn, mn, w_in, 0, 3 * aw, BF16, 1024, 512, "in_proj_qkv")
    fx, f_m = _projection(xn, mn, w_in, 3 * aw, LANES, F32, 1024, LANES, "in_proj_gate")
    w_u = w_in[:, :, 3 * aw + n_heads:].astype(BF16)
    u, u_m = _projection(xn, mn, w_u, 0, 2 * cw, BF16, 1024, 512, "in_proj_glu")

    tq = 512
    bias_f = jnp.zeros((1, LANES), F32).at[0, :n_heads].set(forget_bias[0].astype(F32))
    cx, cm = _forget_cumsum(fx.reshape(b, s, LANES), f_m, bias_f)
    cx = jnp.transpose(cx[:, :, :n_heads], (0, 2, 1)).reshape(b, n_heads, s // tq, tq)
    cm = jnp.zeros((n_heads, 1, LANES), F32).at[:, 0, :N_META].set(jnp.transpose(cm[0, :, :n_heads]))
    kvm = jnp.zeros((LANES, 3 * aw), BF16).at[:N_META].set(qkv_m)
    attn = _attention(qkv.reshape(b, s, 3 * aw), kvm, cx, cm, n_heads, tq)

    conv = _conv_branch(u.reshape(b, s, 2 * cw), u_m, conv_w[0], conv_b[0], conv_ln_g[0], conv_ln_b[0], ts=256)

    h1 = _out_projection(attn.reshape(t_rows, aw), conv.reshape(t_rows, cw), w_out, x2, 1024, 512)

    n_r = N_GROUPS + N_EXPERTS
    wr = jnp.zeros((d, LANES), F32).at[:, :N_GROUPS].set(w_router_group[0]).at[:, N_GROUPS:n_r].set(w_router_expert[0])
    br = jnp.zeros((1, LANES), F32).at[0, :N_GROUPS].set(b_router_group[0]).at[0, N_GROUPS:n_r].set(b_router_expert[0])
    xp, info = _norm_router(h1, ffn_norm_g[0], wr, br, tm=256)

    tm_moe = 512
    n_tiles = (2 * t_rows) // tm_moe + N_EXPERTS
    te, tn, tb, dst = _routing_tables(info[:, 2:4].astype(jnp.int32), tm_moe, n_tiles)
    y = _moe(xp, te, tn, tb, dst, w_expert_gate, w_expert_up, w_expert_down, tm_moe, fc=256)

    out = _combine_final(h1, y.reshape(2, t_rows, d // 2), info, final_norm_g, tm=256)
    return out.reshape(b, s, d)
```

```python
import functools

import jax
import jax.numpy as jnp
from jax import lax
from jax.experimental import pallas as pl
from jax.experimental.pallas import tpu as pltpu

F32 = jnp.float32
BF16 = jnp.bfloat16
U32 = jnp.uint32

N_META = 16
HEAD_DIM = 128
CONV_KERNEL = 31
N_GROUPS = 8
EXPERTS_PER_GROUP = 8
N_EXPERTS = N_GROUPS * EXPERTS_PER_GROUP
RMS_EPS = 1e-6
LN_EPS = 1e-5

LANES = 128
SUBLANES = 8
VMEM_LIMIT = 56 * 1024 * 1024
CONV_HALO = 32
CONV_ROWS = 64
MOE_ROW_GRANULE = 128
DMA_GROUP = 8


def _params(*sem):
    return pltpu.CompilerParams(dimension_semantics=sem, vmem_limit_bytes=VMEM_LIMIT)


def _pack_bf16_pair(lo, hi):
    lo_bits = lax.bitcast_convert_type(lo.astype(BF16).astype(F32), U32)
    hi_bits = lax.bitcast_convert_type(hi.astype(BF16).astype(F32), U32)
    return (lo_bits >> 16) | hi_bits


def _unpack_bf16_pair(p):
    lo = lax.bitcast_convert_type(p << 16, F32)
    hi = lax.bitcast_convert_type(p & jnp.uint32(0xFFFF0000), F32)
    return lo, hi


def _rmsnorm_kernel(x_ref, g_ref, o_ref):
    x = x_ref[...].astype(F32)
    y = x * lax.rsqrt(jnp.mean(x * x, axis=-1, keepdims=True) + RMS_EPS)
    o_ref[...] = (y * g_ref[...]).astype(o_ref.dtype)


def _rmsnorm(x, g, out_dtype, tm):
    m, d = x.shape
    return pl.pallas_call(
        _rmsnorm_kernel,
        grid=(m // tm,),
        in_specs=[pl.BlockSpec((tm, d), lambda i: (i, 0)), pl.BlockSpec((1, d), lambda i: (0, 0))],
        out_specs=pl.BlockSpec((tm, d), lambda i: (i, 0)),
        out_shape=jax.ShapeDtypeStruct((m, d), out_dtype),
        compiler_params=_params("parallel"),
        name="rmsnorm",
    )(x, g.reshape(1, d).astype(F32))


def _proj_kernel(a_ref, am_ref, wt_ref, o_ref, om_ref, wb_ref):
    i = pl.program_id(1)
    tn, k = wt_ref.shape
    kc = min(k, 512)

    @pl.when(i == 0)
    def _():
        for c in range(k // kc):
            wb_ref[c * kc:(c + 1) * kc, :] = wt_ref[:, c * kc:(c + 1) * kc].T.astype(BF16)
        om_ref[...] = jnp.dot(am_ref[...], wb_ref[...], preferred_element_type=F32).astype(om_ref.dtype)

    o_ref[...] = jnp.dot(a_ref[...], wb_ref[...], preferred_element_type=F32).astype(o_ref.dtype)


def _projection(a, am, wt, row0, n_cols, out_dtype, tm, tn, name):
    m, k = a.shape
    mm = am.shape[0]
    assert n_cols % tn == 0 and m % tm == 0 and row0 % SUBLANES == 0
    if row0 % tn == 0:
        w_spec = pl.BlockSpec((tn, k), lambda j, i: (row0 // tn + j, 0))
    else:
        w_spec = pl.BlockSpec((pl.Element(tn), pl.Element(k)), lambda j, i: ((row0 // SUBLANES + j * (tn // SUBLANES)) * SUBLANES, 0))
    return pl.pallas_call(
        _proj_kernel,
        grid=(n_cols // tn, m // tm),
        in_specs=[pl.BlockSpec((tm, k), lambda j, i: (i, 0)),
                  pl.BlockSpec((mm, k), lambda j, i: (0, 0)),
                  w_spec],
        out_specs=[pl.BlockSpec((tm, tn), lambda j, i: (i, j)), pl.BlockSpec((mm, tn), lambda j, i: (0, j))],
        out_shape=[jax.ShapeDtypeStruct((m, n_cols), out_dtype), jax.ShapeDtypeStruct((mm, n_cols), out_dtype)],
        scratch_shapes=[pltpu.VMEM((k, tn), BF16)],
        compiler_params=_params("parallel", "arbitrary"),
        name=name,
    )(a, am, wt)


def _log_sigmoid(z):
    return jnp.minimum(z, 0.0) - jnp.log1p(jnp.exp(-jnp.abs(z)))


def _split3_bf16(x):
    hi = x.astype(BF16)
    r1 = x - hi.astype(F32)
    mid = r1.astype(BF16)
    lo = (r1 - mid.astype(F32)).astype(BF16)
    return hi, mid, lo


def _cumsum_rows(l, tril):
    dot = functools.partial(jnp.dot, preferred_element_type=F32)
    hi, mid, lo = _split3_bf16(l)
    return dot(tril, hi) + dot(tril, mid) + dot(tril, lo)


def _tril(n):
    r = lax.broadcasted_iota(jnp.int32, (n, n), 0)
    c = lax.broadcasted_iota(jnp.int32, (n, n), 1)
    return jnp.where(r >= c, 1.0, 0.0).astype(BF16)


def _cumsum_kernel(fx_ref, fm_ref, b_ref, cx_ref, cm_ref, *, blk):
    bias = b_ref[...]
    cm = _cumsum_rows(_log_sigmoid(fm_ref[...] + bias), _tril(N_META))
    cm_ref[0] = cm
    carry = cm[N_META - 1:N_META, :]
    tril = _tril(blk)
    for j in range(fx_ref.shape[1] // blk):
        l = _log_sigmoid(fx_ref[0, j * blk:(j + 1) * blk, :] + bias)
        c = _cumsum_rows(l, tril) + carry
        cx_ref[0, j * blk:(j + 1) * blk, :] = c
        carry = c[blk - 1:blk, :]


def _forget_cumsum(fx, fm, bias):
    b, s, _ = fx.shape
    return pl.pallas_call(
        functools.partial(_cumsum_kernel, blk=256),
        grid=(b,),
        in_specs=[pl.BlockSpec((1, s, LANES), lambda i: (i, 0, 0)),
                  pl.BlockSpec((N_META, LANES), lambda i: (0, 0)),
                  pl.BlockSpec((1, LANES), lambda i: (0, 0))],
        out_specs=[pl.BlockSpec((1, s, LANES), lambda i: (i, 0, 0)),
                   pl.BlockSpec((1, N_META, LANES), lambda i: (i, 0, 0))],
        out_shape=[jax.ShapeDtypeStruct((b, s, LANES), F32), jax.ShapeDtypeStruct((b, N_META, LANES), F32)],
        compiler_params=_params("parallel"),
        name="forget_cumsum",
    )(fx, fm, bias)


def _attn_kernel(q_ref, k_ref, v_ref, km_ref, vm_ref, cx_ref, cm_ref, o_ref, *, tq):
    i = pl.program_id(2)
    q = q_ref[0]
    scale = HEAD_DIM ** -0.5

    def scores(kb, cb):
        s = lax.dot_general(q, kb, (((1,), (1,)), ((), ())), preferred_element_type=F32)
        return s * scale - cb

    s = scores(km_ref[...], cm_ref[0])
    col = lax.broadcasted_iota(jnp.int32, s.shape, 1)
    s = jnp.where(col < N_META, s, -jnp.inf)
    m = jnp.max(s, axis=-1, keepdims=True)
    p = jnp.exp(s - m)
    l = jnp.sum(p, axis=-1, keepdims=True)
    acc = jnp.dot(p.astype(BF16), vm_ref[...], preferred_element_type=F32)

    def update(s, vb, m, l, acc):
        m_new = jnp.maximum(m, jnp.max(s, axis=-1, keepdims=True))
        alpha = jnp.exp(m - m_new)
        p = jnp.exp(s - m_new)
        l = alpha * l + jnp.sum(p, axis=-1, keepdims=True)
        acc = alpha * acc + jnp.dot(p.astype(BF16), vb, preferred_element_type=F32)
        return m_new, l, acc

    def block(j):
        off = pl.multiple_of(j * tq, tq)
        return (scores(k_ref[0, pl.ds(off, tq), :], cx_ref[0, 0, pl.ds(j, 1), :]), v_ref[0, pl.ds(off, tq), :])

    def body(j, carry):
        s, vb = block(j)
        return update(s, vb, *carry)

    m, l, acc = lax.fori_loop(0, i, body, (m, l, acc))
    s, vb = block(i)
    row = lax.broadcasted_iota(jnp.int32, s.shape, 0)
    col = lax.broadcasted_iota(jnp.int32, s.shape, 1)
    s = jnp.where(row >= col, s, -jnp.inf)
    m, l, acc = update(s, vb, m, l, acc)
    o_ref[0] = (acc / l).astype(o_ref.dtype)


def _attention(qkv, kvm, cx, cm, n_heads, tq):
    b, s, _ = qkv.shape
    nk = s // tq
    blk = lambda f: pl.BlockSpec((1, tq, HEAD_DIM), f)
    return pl.pallas_call(
        functools.partial(_attn_kernel, tq=tq),
        grid=(b, n_heads, nk),
        in_specs=[blk(lambda bi, h, i: (bi, i, h)),
                  pl.BlockSpec((1, s, HEAD_DIM), lambda bi, h, i: (bi, 0, n_heads + h)),
                  pl.BlockSpec((1, s, HEAD_DIM), lambda bi, h, i: (bi, 0, 2 * n_heads + h)),
                  pl.BlockSpec((LANES, HEAD_DIM), lambda bi, h, i: (0, n_heads + h)),
                  pl.BlockSpec((LANES, HEAD_DIM), lambda bi, h, i: (0, 2 * n_heads + h)),
                  pl.BlockSpec((1, 1, nk, tq), lambda bi, h, i: (bi, h, 0, 0)),
                  pl.BlockSpec((1, 1, LANES), lambda bi, h, i: (h, 0, 0))],
        out_specs=blk(lambda bi, h, i: (bi, i, h)),
        out_shape=jax.ShapeDtypeStruct((b, s, n_heads * HEAD_DIM), BF16),
        compiler_params=_params("parallel", "parallel", "parallel"),
        name="fox_attention",
    )(qkv, qkv, qkv, kvm, kvm, cx, cm)


def _conv_kernel(ua_ref, ug_ref, ma_ref, mg_ref, w_ref, cb_ref, g_ref, b_ref, o_ref, hb, hs, cbuf, *, ts):
    t = pl.program_id(1)
    nch = hb.shape[0]
    off = CONV_HALO - (CONV_KERNEL - 1)
    span = ts + CONV_HALO - SUBLANES
    lanes = lambda j: slice(j * LANES, (j + 1) * LANES)

    @pl.when(t == 0)
    def _():
        for j in range(nch):
            hb[j, 0:CONV_HALO - N_META, :] = jnp.zeros((CONV_HALO - N_META, LANES), F32)
            hb[j, CONV_HALO - N_META:CONV_HALO, :] = (
                ma_ref[:, lanes(j)].astype(F32) * jax.nn.sigmoid(mg_ref[:, lanes(j)].astype(F32)))

    for j in range(nch):
        hb[j, CONV_HALO:CONV_HALO + ts, :] = (
            ua_ref[0, :, lanes(j)].astype(F32) * jax.nn.sigmoid(ug_ref[0, :, lanes(j)].astype(F32)))

    def lane_chunk(j, carry):
        for s in range(1, SUBLANES):
            hs[s - 1] = hb[j, pl.ds(s, span), :]

        def row_chunk(i, c2):
            r0 = pl.multiple_of(i * CONV_ROWS, CONV_ROWS)
            tile_rows = lambda v: jnp.concatenate([v] * (CONV_ROWS // SUBLANES), axis=0)
            acc = tile_rows(cb_ref[j])
            for k in range(CONV_KERNEL):
                a, s = divmod(off + k, SUBLANES)
                rows = pl.ds(r0 + SUBLANES * a, CONV_ROWS)
                src = hb[j, rows, :] if s == 0 else hs[s - 1, rows, :]
                acc = acc + tile_rows(w_ref[j, k]) * src
            cbuf[j, pl.ds(r0, CONV_ROWS), :] = acc
            return c2

        lax.fori_loop(0, ts // CONV_ROWS, row_chunk, 0)
        return carry

    lax.fori_loop(0, nch, lane_chunk, 0)

    for j in range(nch):
        hb[j, 0:CONV_HALO, :] = hb[j, ts:ts + CONV_HALO, :]

    c = cbuf[...]
    n_ch = nch * LANES
    mu = jnp.sum(jnp.sum(c, axis=0), axis=-1, keepdims=True) / n_ch
    xc = c - mu[None]
    var = jnp.sum(jnp.sum(xc * xc, axis=0), axis=-1, keepdims=True) / n_ch
    y = xc * lax.rsqrt(var + LN_EPS)[None] * g_ref[...] + b_ref[...]
    y = y * jax.nn.sigmoid(y)
    for j in range(nch):
        o_ref[0, :, lanes(j)] = y[j].astype(o_ref.dtype)


def _conv_branch(u, um, conv_w, conv_b, ln_g, ln_b, ts):
    b, s, c2 = u.shape
    c = c2 // 2
    nch = c // LANES
    chunked = lambda v: jnp.transpose(v.astype(F32).reshape(-1, nch, LANES), (1, 0, 2))
    w_rep = jnp.broadcast_to(chunked(conv_w)[:, :, None, :], (nch, CONV_KERNEL, SUBLANES, LANES))
    cb_rep = jnp.broadcast_to(chunked(conv_b), (nch, SUBLANES, LANES))
    full = lambda shape: pl.BlockSpec(shape, lambda bi, t: (0,) * len(shape))
    return pl.pallas_call(
        functools.partial(_conv_kernel, ts=ts),
        grid=(b, s // ts),
        in_specs=[pl.BlockSpec((1, ts, c), lambda bi, t: (bi, t, 0)),
                  pl.BlockSpec((1, ts, c), lambda bi, t: (bi, t, 1)),
                  pl.BlockSpec((N_META, c), lambda bi, t: (0, 0)),
                  pl.BlockSpec((N_META, c), lambda bi, t: (0, 1)),
                  full((nch, CONV_KERNEL, SUBLANES, LANES)),
                  full((nch, SUBLANES, LANES)),
                  full((nch, 1, LANES)),
                  full((nch, 1, LANES))],
        out_specs=pl.BlockSpec((1, ts, c), lambda bi, t: (bi, t, 0)),
        out_shape=jax.ShapeDtypeStruct((b, s, c), BF16),
        scratch_shapes=[pltpu.VMEM((nch, ts + CONV_HALO, LANES), F32),
                        pltpu.VMEM((SUBLANES - 1, ts + CONV_HALO - SUBLANES, LANES), F32),
                        pltpu.VMEM((nch, ts, LANES), F32)],
        compiler_params=_params("parallel", "arbitrary"),
        name="conformer_conv",
    )(u, u, um, um, w_rep, cb_rep, chunked(ln_g), chunked(ln_b))


def _outproj_kernel(a_ref, c_ref, wa_ref, wc_ref, x_ref, o_ref, wab_ref, wcb_ref):
    @pl.when(pl.program_id(1) == 0)
    def _():
        wab_ref[...] = wa_ref[0].astype(BF16)
        wcb_ref[...] = wc_ref[0].astype(BF16)

    acc = jnp.dot(a_ref[...], wab_ref[...], preferred_element_type=F32)
    acc = acc + jnp.dot(c_ref[...], wcb_ref[...], preferred_element_type=F32)
    o_ref[...] = x_ref[...] + acc


def _out_projection(a, c, w, x, tm, tn):
    m, ka = a.shape
    kc = c.shape[1]
    n = w.shape[2]
    assert ka == kc
    return pl.pallas_call(
        _outproj_kernel,
        grid=(n // tn, m // tm),
        in_specs=[pl.BlockSpec((tm, ka), lambda j, i: (i, 0)),
                  pl.BlockSpec((tm, kc), lambda j, i: (i, 0)),
                  pl.BlockSpec((1, ka, tn), lambda j, i: (0, 0, j)),
                  pl.BlockSpec((1, kc, tn), lambda j, i: (0, 1, j)),
                  pl.BlockSpec((tm, tn), lambda j, i: (i, j))],
        out_specs=pl.BlockSpec((tm, tn), lambda j, i: (i, j)),
        out_shape=jax.ShapeDtypeStruct((m, n), F32),
        scratch_shapes=[pltpu.VMEM((ka, tn), BF16), pltpu.VMEM((kc, tn), BF16)],
        compiler_params=_params("parallel", "arbitrary"),
        name="out_projection",
    )(a, c, w, w, x)


def _router_kernel(h_ref, g_ref, wr_ref, br_ref, xp_ref, info_ref):
    h = h_ref[...]
    xn = h * lax.rsqrt(jnp.mean(h * h, axis=-1, keepdims=True) + RMS_EPS) * g_ref[...]
    half = xn.shape[1] // 2
    xp_ref[...] = _pack_bf16_pair(xn[:, :half], xn[:, half:])
    logits = jnp.dot(xn, wr_ref[...], precision=lax.Precision.HIGHEST, preferred_element_type=F32) + br_ref[...]
    lane = lax.broadcasted_iota(jnp.int32, logits.shape, 1).astype(F32)
    rmax = lambda v: jnp.max(v, axis=-1, keepdims=True)
    rmin = lambda v: jnp.min(v, axis=-1, keepdims=True)
    rsum = lambda v: jnp.sum(v, axis=-1, keepdims=True)
    far = float(LANES)

    is_g = lane < N_GROUPS
    lg = jnp.where(is_g, logits, -jnp.inf)
    mg = rmax(lg)
    g_idx = rmin(jnp.where(lg == mg, lane, far))
    g_w = 1.0 / rsum(jnp.where(is_g, jnp.exp(lg - mg), 0.0))

    lo = N_GROUPS + EXPERTS_PER_GROUP * g_idx
    in_e = jnp.logical_and(lane >= lo, lane < lo + EXPERTS_PER_GROUP)
    le = jnp.where(in_e, logits, -jnp.inf)
    pe = jnp.where(in_e, jnp.exp(le - rmax(le)), 0.0)
    prob = jnp.where(in_e, pe / rsum(pe), -1.0)
    p1 = rmax(prob)
    i1 = rmin(jnp.where(prob == p1, lane, far))
    prob2 = jnp.where(lane == i1, -1.0, prob)
    p2 = rmax(prob2)
    i2 = rmin(jnp.where(prob2 == p2, lane, far))
    den = p1 + p2
    w1 = g_w * (p1 / den)
    w2 = g_w * (p2 / den)
    info = jnp.where(lane == 0, w1, jnp.where(lane == 1, w2, jnp.where(
        lane == 2, i1 - N_GROUPS, jnp.where(lane == 3, i2 - N_GROUPS, 0.0))))
    info_ref[...] = info


def _norm_router(h, g, wr, br, tm):
    m, d = h.shape
    return pl.pallas_call(
        _router_kernel,
        grid=(m // tm,),
        in_specs=[pl.BlockSpec((tm, d), lambda i: (i, 0)),
                  pl.BlockSpec((1, d), lambda i: (0, 0)),
                  pl.BlockSpec((d, LANES), lambda i: (0, 0)),
                  pl.BlockSpec((1, LANES), lambda i: (0, 0))],
        out_specs=[pl.BlockSpec((tm, d // 2), lambda i: (i, 0)), pl.BlockSpec((tm, LANES), lambda i: (i, 0))],
        out_shape=[jax.ShapeDtypeStruct((m, d // 2), U32), jax.ShapeDtypeStruct((m, LANES), F32)],
        compiler_params=_params("parallel"),
        name="norm_router",
    )(h, g.reshape(1, d).astype(F32), wr, br)


def _moe_kernel(te_ref, tn_ref, tb_ref, asg_ref, xp_hbm, wg_ref, wu_ref, wd_ref, y_hbm,
                xq, xb, acc, ystage, pend, gsem, ssem, *, t_rows):
    t = pl.program_id(0)
    f = pl.program_id(1)
    n_tiles = pl.num_programs(0)
    nf = pl.num_programs(1)
    n = tn_ref[t]
    tm, half = ystage.shape
    groups = lambda cnt: (cnt + DMA_GROUP - 1) // DMA_GROUP

    def gather_copy(r, tok):
        return pltpu.make_async_copy(xp_hbm.at[pl.ds(tok, 1), :], xq.at[pl.ds(r, 1), :], gsem)

    def scatter_copy(r, row):
        return pltpu.make_async_copy(ystage.at[pl.ds(r, 1), :], y_hbm.at[pl.ds(row, 1), :], ssem)

    def start_gather(tile):
        base = tb_ref[tile]
        cnt = tn_ref[tile]

        def body(g, carry):
            for u in range(DMA_GROUP):
                r = g * DMA_GROUP + u
                gather_copy(r, asg_ref[base + jnp.minimum(r, cnt - 1)] >> 1).start()
            return carry
        lax.fori_loop(0, groups(cnt), body, 0)

    def wait_gather(cnt):
        def body(g, carry):
            for u in range(DMA_GROUP):
                gather_copy(g * DMA_GROUP + u, 0).wait()
            return carry
        lax.fori_loop(0, groups(cnt), body, 0)

    def start_scatter():
        base = tb_ref[t]

        def body(g, carry):
            for u in range(DMA_GROUP):
                r = g * DMA_GROUP + u
                a = asg_ref[base + jnp.minimum(r, n - 1)]
                row = jnp.where(r < n, (a & 1) * t_rows + (a >> 1), 2 * t_rows + u)
                scatter_copy(r, row).start()
            return carry
        lax.fori_loop(0, groups(n), body, 0)
        pend[0] = groups(n)

    def wait_scatter():
        def body(g, carry):
            for u in range(DMA_GROUP):
                scatter_copy(g * DMA_GROUP + u, 0).wait()
            return carry
        lax.fori_loop(0, pend[0], body, 0)
        pend[0] = 0

    @pl.when(jnp.logical_and(t == 0, f == 0))
    def _():
        xq[...] = jnp.zeros(xq.shape, xq.dtype)
        pend[0] = 0
        spare = pltpu.make_async_copy(xq.at[pl.ds(0, DMA_GROUP), :], y_hbm.at[pl.ds(2 * t_rows, DMA_GROUP), :], ssem)
        spare.start()
        spare.wait()
        start_gather(0)

    n_granules = (n + MOE_ROW_GRANULE - 1) // MOE_ROW_GRANULE
    row_counts = [k * MOE_ROW_GRANULE for k in range(1, tm // MOE_ROW_GRANULE + 1)]

    def unpack_rows(m):
        lo, hi = _unpack_bf16_pair(xq[0:m, :])
        xb[0:m, :half] = lo.astype(BF16)
        xb[0:m, half:] = hi.astype(BF16)

    @pl.when(jnp.logical_and(f == 0, n > 0))
    def _():
        wait_gather(n)
        for m in row_counts:
            pl.when(n_granules * MOE_ROW_GRANULE == m)(functools.partial(unpack_rows, m))

    @pl.when(jnp.logical_and(f == 0, t + 1 < n_tiles))
    def _():
        start_gather(t + 1)

    def ffn_rows(m):
        x = xb[0:m, :]
        g = jnp.dot(x, wg_ref[0].astype(BF16), preferred_element_type=F32)
        u = jnp.dot(x, wu_ref[0].astype(BF16), preferred_element_type=F32)
        hid = ((g * jax.nn.sigmoid(g)) * u).astype(BF16)
        part = jnp.dot(hid, wd_ref[0].astype(BF16), preferred_element_type=F32)

        @pl.when(f == 0)
        def _():
            acc[0:m, :] = part

        @pl.when(f > 0)
        def _():
            acc[0:m, :] = acc[0:m, :] + part

        @pl.when(f == nf - 1)
        def _():
            wait_scatter()
            ystage[0:m, :] = _pack_bf16_pair(acc[0:m, :half], acc[0:m, half:])
            start_scatter()

    for m in row_counts:
        pl.when(n_granules * MOE_ROW_GRANULE == m)(functools.partial(ffn_rows, m))

    @pl.when(jnp.logical_and(f == nf - 1, t == n_tiles - 1))
    def _():
        wait_scatter()


def _moe(xp, tile_e, tile_n, tile_base, asg, w_gate, w_up, w_down, tm, fc):
    assert tm % MOE_ROW_GRANULE == 0 and MOE_ROW_GRANULE % DMA_GROUP == 0
    t_rows, half = xp.shape
    d = 2 * half
    n_tiles = tile_e.shape[0]
    ff = w_gate.shape[3]
    nf = ff // fc

    def fidx(f, tn, t):
        return jnp.where(tn[t] > 0, f, nf - 1)

    grid_spec = pltpu.PrefetchScalarGridSpec(
        num_scalar_prefetch=4,
        grid=(n_tiles, nf),
        in_specs=[pl.BlockSpec(memory_space=pl.ANY),
                  pl.BlockSpec((1, 1, d, fc), lambda t, f, te, tn, tb, ds: (0, te[t], 0, fidx(f, tn, t))),
                  pl.BlockSpec((1, 1, d, fc), lambda t, f, te, tn, tb, ds: (0, te[t], 0, fidx(f, tn, t))),
                  pl.BlockSpec((1, 1, fc, d), lambda t, f, te, tn, tb, ds: (0, te[t], fidx(f, tn, t), 0))],
        out_specs=pl.BlockSpec(memory_space=pl.ANY),
        scratch_shapes=[pltpu.VMEM((tm, half), U32), pltpu.VMEM((tm, d), BF16), pltpu.VMEM((tm, d), F32),
                        pltpu.VMEM((tm, half), U32), pltpu.SMEM((1,), jnp.int32),
                        pltpu.SemaphoreType.DMA(()), pltpu.SemaphoreType.DMA(())],
    )
    return pl.pallas_call(
        functools.partial(_moe_kernel_4d, t_rows=t_rows),
        grid_spec=grid_spec,
        out_shape=jax.ShapeDtypeStruct((2 * t_rows + DMA_GROUP, half), U32),
        compiler_params=_params("arbitrary", "arbitrary"),
        name="sparse_moe",
    )(tile_e, tile_n, tile_base, asg, xp, w_gate, w_up, w_down)


def _moe_kernel_4d(te_ref, tn_ref, tb_ref, asg_ref, xp_hbm, wg_ref, wu_ref, wd_ref, *rest, t_rows):
    _moe_kernel(te_ref, tn_ref, tb_ref, asg_ref, xp_hbm, wg_ref.at[0], wu_ref.at[0], wd_ref.at[0], *rest,
                t_rows=t_rows)


def _routing_tables(e_idx, tm, n_tiles):
    flat_e = e_idx.reshape(-1)
    n_assign = flat_e.shape[0]
    onehot = (flat_e[:, None] == jnp.arange(N_EXPERTS, dtype=jnp.int32)[None, :]).astype(jnp.int32)
    csum = jnp.cumsum(onehot, axis=0)
    rank = jnp.sum(csum * onehot, axis=1) - 1
    counts = csum[-1]
    row_end = jnp.cumsum(counts)
    row_start = row_end - counts
    tiles_per_e = (counts + tm - 1) // tm
    tile_end = jnp.cumsum(tiles_per_e)
    tile_start = tile_end - tiles_per_e
    total = tile_end[-1]
    pos = jnp.sum(onehot * row_start[None, :], axis=1) + rank
    asg = jnp.zeros((n_assign,), jnp.int32).at[pos].set(jnp.arange(n_assign, dtype=jnp.int32), unique_indices=True)
    tid = jnp.arange(n_tiles, dtype=jnp.int32)
    te = jnp.sum((tid[:, None] >= tile_end[None, :]).astype(jnp.int32), axis=1)
    te = jnp.minimum(te, N_EXPERTS - 1)
    active = tid < total
    first = (tid - tile_start[te]) * tm
    tn = jnp.where(active, jnp.clip(counts[te] - first, 0, tm), 0).astype(jnp.int32)
    tb = jnp.where(active, row_start[te] + first, 0).astype(jnp.int32)
    last_e = te[jnp.maximum(total - 1, 0)]
    te = jnp.where(active, te, last_e).astype(jnp.int32)
    return te, tn, tb, asg


def _final_kernel(h_ref, y0_ref, y1_ref, info_ref, g_ref, o_ref):
    w = info_ref[...]
    w0 = w[:, 0:1]
    w1 = w[:, 1:2]
    half = y0_ref.shape[1]
    y0_lo, y0_hi = _unpack_bf16_pair(y0_ref[...])
    y1_lo, y1_hi = _unpack_bf16_pair(y1_ref[...])
    h_lo = h_ref[:, :half] + w0 * y0_lo + w1 * y1_lo
    h_hi = h_ref[:, half:] + w0 * y0_hi + w1 * y1_hi
    ms = (jnp.sum(h_lo * h_lo, axis=-1, keepdims=True) + jnp.sum(h_hi * h_hi, axis=-1, keepdims=True)) / (2 * half)
    r = lax.rsqrt(ms + RMS_EPS)
    o_ref[:, :half] = h_lo * r * g_ref[:, :half]
    o_ref[:, half:] = h_hi * r * g_ref[:, half:]


def _combine_final(h, y, info, g, tm):
    m, d = h.shape
    return pl.pallas_call(
        _final_kernel,
        grid=(m // tm,),
        in_specs=[pl.BlockSpec((tm, d), lambda i: (i, 0)),
                  pl.BlockSpec((tm, d // 2), lambda i: (i, 0)),
                  pl.BlockSpec((tm, d // 2), lambda i: (m // tm + i, 0)),
                  pl.BlockSpec((tm, LANES), lambda i: (i, 0)),
                  pl.BlockSpec((1, d), lambda i: (0, 0))],
        out_specs=pl.BlockSpec((tm, d), lambda i: (i, 0)),
        out_shape=jax.ShapeDtypeStruct((m, d), F32),
        compiler_params=_params("parallel"),
        name="combine_final",
    )(h, y, y, info, g.reshape(1, d).astype(F32))


def kernel(x, meta_tokens, attn_norm_g, w_in, forget_bias, conv_w, conv_b, conv_ln_g, conv_ln_b, w_out,
           ffn_norm_g, w_router_group, b_router_group, w_router_expert, b_router_expert,
           w_expert_gate, w_expert_up, w_expert_down, final_norm_g):
    b, s, d = x.shape
    depth = w_in.shape[0]
    assert depth == 1, "one layer: the meta rows are not carried past the mixer"
    n_heads = forget_bias.shape[1]
    aw = n_heads * HEAD_DIM
    cw = conv_w.shape[2]
    t_rows = b * s
    x2 = x.reshape(t_rows, d)

    xn = _rmsnorm(x2, attn_norm_g[0], BF16, tm=512)
    mn = _rmsnorm(meta_tokens.astype(x.dtype), attn_norm_g[0], BF16, tm=N_META)

    w_t = jnp.transpose(w_in[0])
    qkv, qkv_m = _projection(xn, mn, w_t, 0, 3 * aw, BF16, 1024, 512, "in_proj_qkv")
    fx, f_m = _projection(xn, mn, w_t, 3 * aw, LANES, F32, 1024, LANES, "in_proj_gate")
    u, u_m = _projection(xn, mn, w_t, 3 * aw + n_heads, 2 * cw, BF16, 1024, 512, "in_proj_glu")

    tq = 512
    bias_f = jnp.zeros((1, LANES), F32).at[0, :n_heads].set(forget_bias[0].astype(F32))
    cx, cm = _forget_cumsum(fx.reshape(b, s, LANES), f_m, bias_f)
    cx = jnp.transpose(cx[:, :, :n_heads], (0, 2, 1)).reshape(b, n_heads, s // tq, tq)
    cm = jnp.zeros((n_heads, 1, LANES), F32).at[:, 0, :N_META].set(jnp.transpose(cm[0, :, :n_heads]))
    kvm = jnp.zeros((LANES, 3 * aw), BF16).at[:N_META].set(qkv_m)
    attn = _attention(qkv.reshape(b, s, 3 * aw), kvm, cx, cm, n_heads, tq)

    conv = _conv_branch(u.reshape(b, s, 2 * cw), u_m, conv_w[0], conv_b[0], conv_ln_g[0], conv_ln_b[0], ts=256)

    h1 = _out_projection(attn.reshape(t_rows, aw), conv.reshape(t_rows, cw), w_out, x2, 1024, 512)

    n_r = N_GROUPS + N_EXPERTS
    wr = jnp.zeros((d, LANES), F32).at[:, :N_GROUPS].set(w_router_group[0]).at[:, N_GROUPS:n_r].set(w_router_expert[0])
    br = jnp.zeros((1, LANES), F32).at[0, :N_GROUPS].set(b_router_group[0]).at[0, N_GROUPS:n_r].set(b_router_expert[0])
    xp, info = _norm_router(h1, ffn_norm_g[0], wr, br, tm=256)

    tm_moe = 512
    n_tiles = (2 * t_rows) // tm_moe + N_EXPERTS
    te, tn, tb, asg = _routing_tables(info[:, 2:4].astype(jnp.int32), tm_moe, n_tiles)
    y = _moe(xp, te, tn, tb, asg, w_expert_gate, w_expert_up, w_expert_down, tm_moe, fc=256)

    out = _combine_final(h1, y, info, final_norm_g, tm=256)
    return out.reshape(b, s, d)
```

```python
import functools

import jax
import jax.numpy as jnp
from jax import lax
from jax.experimental import pallas as pl
from jax.experimental.pallas import tpu as pltpu

F32 = jnp.float32
BF16 = jnp.bfloat16
U32 = jnp.uint32

N_META = 16
HEAD_DIM = 128
CONV_KERNEL = 31
N_GROUPS = 8
EXPERTS_PER_GROUP = 8
N_EXPERTS = N_GROUPS * EXPERTS_PER_GROUP
RMS_EPS = 1e-6
LN_EPS = 1e-5

LANES = 128
SUBLANES = 8
VMEM_LIMIT = 56 * 1024 * 1024
CONV_HALO = 32
CONV_ROWS = 64
MOE_ROW_GRANULE = 64
DMA_GROUP = 8


def _params(*sem):
    return pltpu.CompilerParams(dimension_semantics=sem, vmem_limit_bytes=VMEM_LIMIT)


def _pack_bf16_pair(lo, hi):
    lo_bits = lax.bitcast_convert_type(lo.astype(BF16).astype(F32), U32)
    hi_bits = lax.bitcast_convert_type(hi.astype(BF16).astype(F32), U32)
    return (lo_bits >> 16) | hi_bits


def _unpack_bf16_pair(p):
    lo = lax.bitcast_convert_type(p << 16, F32)
    hi = lax.bitcast_convert_type(p & jnp.uint32(0xFFFF0000), F32)
    return lo, hi


def _rmsnorm_kernel(x_ref, g_ref, o_ref):
    x = x_ref[...].astype(F32)
    y = x * lax.rsqrt(jnp.mean(x * x, axis=-1, keepdims=True) + RMS_EPS)
    o_ref[...] = (y * g_ref[...]).astype(o_ref.dtype)


def _rmsnorm(x, g, out_dtype, tm):
    m, d = x.shape
    return pl.pallas_call(
        _rmsnorm_kernel,
        grid=(m // tm,),
        in_specs=[pl.BlockSpec((tm, d), lambda i: (i, 0)), pl.BlockSpec((1, d), lambda i: (0, 0))],
        out_specs=pl.BlockSpec((tm, d), lambda i: (i, 0)),
        out_shape=jax.ShapeDtypeStruct((m, d), out_dtype),
        compiler_params=_params("parallel"),
        name="rmsnorm",
    )(x, g.reshape(1, d).astype(F32))


def _proj_kernel(a_ref, am_ref, wt_ref, o_ref, om_ref, wb_ref):
    i = pl.program_id(1)
    tn, k = wt_ref.shape
    kc = min(k, 512)

    @pl.when(i == 0)
    def _():
        for c in range(k // kc):
            wb_ref[c * kc:(c + 1) * kc, :] = wt_ref[:, c * kc:(c + 1) * kc].T.astype(BF16)
        om_ref[...] = jnp.dot(am_ref[...], wb_ref[...], preferred_element_type=F32).astype(om_ref.dtype)

    o_ref[...] = jnp.dot(a_ref[...], wb_ref[...], preferred_element_type=F32).astype(o_ref.dtype)


def _projection(a, am, wt, row0, n_cols, out_dtype, tm, tn, name):
    m, k = a.shape
    mm = am.shape[0]
    assert n_cols % tn == 0 and m % tm == 0 and row0 % SUBLANES == 0
    if row0 % tn == 0:
        w_spec = pl.BlockSpec((tn, k), lambda j, i: (row0 // tn + j, 0))
    else:
        w_spec = pl.BlockSpec((pl.Element(tn), pl.Element(k)), lambda j, i: ((row0 // SUBLANES + j * (tn // SUBLANES)) * SUBLANES, 0))
    return pl.pallas_call(
        _proj_kernel,
        grid=(n_cols // tn, m // tm),
        in_specs=[pl.BlockSpec((tm, k), lambda j, i: (i, 0)),
                  pl.BlockSpec((mm, k), lambda j, i: (0, 0)),
                  w_spec],
        out_specs=[pl.BlockSpec((tm, tn), lambda j, i: (i, j)), pl.BlockSpec((mm, tn), lambda j, i: (0, j))],
        out_shape=[jax.ShapeDtypeStruct((m, n_cols), out_dtype), jax.ShapeDtypeStruct((mm, n_cols), out_dtype)],
        scratch_shapes=[pltpu.VMEM((k, tn), BF16)],
        compiler_params=_params("parallel", "arbitrary"),
        name=name,
    )(a, am, wt)


def _log_sigmoid(z):
    return jnp.minimum(z, 0.0) - jnp.log1p(jnp.exp(-jnp.abs(z)))


def _split3_bf16(x):
    hi = x.astype(BF16)
    r1 = x - hi.astype(F32)
    mid = r1.astype(BF16)
    lo = (r1 - mid.astype(F32)).astype(BF16)
    return hi, mid, lo


def _cumsum_rows(l, tril):
    dot = functools.partial(jnp.dot, preferred_element_type=F32)
    hi, mid, lo = _split3_bf16(l)
    return dot(tril, hi) + dot(tril, mid) + dot(tril, lo)


def _tril(n):
    r = lax.broadcasted_iota(jnp.int32, (n, n), 0)
    c = lax.broadcasted_iota(jnp.int32, (n, n), 1)
    return jnp.where(r >= c, 1.0, 0.0).astype(BF16)


def _cumsum_kernel(fx_ref, fm_ref, b_ref, cx_ref, cm_ref, *, blk):
    bias = b_ref[...]
    cm = _cumsum_rows(_log_sigmoid(fm_ref[...] + bias), _tril(N_META))
    cm_ref[0] = cm
    carry = cm[N_META - 1:N_META, :]
    tril = _tril(blk)
    for j in range(fx_ref.shape[1] // blk):
        l = _log_sigmoid(fx_ref[0, j * blk:(j + 1) * blk, :] + bias)
        c = _cumsum_rows(l, tril) + carry
        cx_ref[0, j * blk:(j + 1) * blk, :] = c
        carry = c[blk - 1:blk, :]


def _forget_cumsum(fx, fm, bias):
    b, s, _ = fx.shape
    return pl.pallas_call(
        functools.partial(_cumsum_kernel, blk=256),
        grid=(b,),
        in_specs=[pl.BlockSpec((1, s, LANES), lambda i: (i, 0, 0)),
                  pl.BlockSpec((N_META, LANES), lambda i: (0, 0)),
                  pl.BlockSpec((1, LANES), lambda i: (0, 0))],
        out_specs=[pl.BlockSpec((1, s, LANES), lambda i: (i, 0, 0)),
                   pl.BlockSpec((1, N_META, LANES), lambda i: (i, 0, 0))],
        out_shape=[jax.ShapeDtypeStruct((b, s, LANES), F32), jax.ShapeDtypeStruct((b, N_META, LANES), F32)],
        compiler_params=_params("parallel"),
        name="forget_cumsum",
    )(fx, fm, bias)


def _attn_kernel(q_ref, k_ref, v_ref, km_ref, vm_ref, cx_ref, cm_ref, o_ref, *, tq):
    scale = HEAD_DIM ** -0.5
    hq = tq // 2

    def scores(q, kb, cb):
        s = lax.dot_general(q, kb, (((1,), (1,)), ((), ())), preferred_element_type=F32)
        return s * scale - cb

    def update(s, vb, m, l, acc):
        m_new = jnp.maximum(m, jnp.max(s, axis=-1, keepdims=True))
        alpha = jnp.exp(m - m_new)
        p = jnp.exp(s - m_new)
        l = alpha * l + jnp.sum(p, axis=-1, keepdims=True)
        acc = alpha * acc + jnp.dot(p.astype(BF16), vb, preferred_element_type=F32)
        return m_new, l, acc

    def q_tile(i, carry):
        q0 = pl.multiple_of(i * tq, tq)
        q = q_ref[0, pl.ds(q0, tq), :]

        s = scores(q, km_ref[...], cm_ref[0])
        col = lax.broadcasted_iota(jnp.int32, s.shape, 1)
        s = jnp.where(col < N_META, s, -jnp.inf)
        m = jnp.max(s, axis=-1, keepdims=True)
        p = jnp.exp(s - m)
        l = jnp.sum(p, axis=-1, keepdims=True)
        acc = jnp.dot(p.astype(BF16), vm_ref[...], preferred_element_type=F32)

        def body(j, st):
            off = pl.multiple_of(j * tq, tq)
            s = scores(q, k_ref[0, pl.ds(off, tq), :], cx_ref[0, 0, pl.ds(j, 1), :])
            return update(s, v_ref[0, pl.ds(off, tq), :], *st)

        m, l, acc = lax.fori_loop(0, i, body, (m, l, acc))

        cb = cx_ref[0, 0, pl.ds(i, 1), :]
        s = scores(q[:hq], k_ref[0, pl.ds(q0, hq), :], cb[:, :hq])
        row = lax.broadcasted_iota(jnp.int32, s.shape, 0)
        col = lax.broadcasted_iota(jnp.int32, s.shape, 1)
        top = update(jnp.where(row >= col, s, -jnp.inf), v_ref[0, pl.ds(q0, hq), :], m[:hq], l[:hq], acc[:hq])
        s = scores(q[hq:], k_ref[0, pl.ds(q0, tq), :], cb)
        row = lax.broadcasted_iota(jnp.int32, s.shape, 0) + hq
        col = lax.broadcasted_iota(jnp.int32, s.shape, 1)
        bot = update(jnp.where(row >= col, s, -jnp.inf), v_ref[0, pl.ds(q0, tq), :], m[hq:], l[hq:], acc[hq:])
        o_ref[0, pl.ds(q0, hq), :] = (top[2] / top[1]).astype(o_ref.dtype)
        o_ref[0, pl.ds(q0 + hq, hq), :] = (bot[2] / bot[1]).astype(o_ref.dtype)
        return carry

    lax.fori_loop(0, q_ref.shape[1] // tq, q_tile, 0)


def _attention(qkv, kvm, cx, cm, n_heads, tq):
    b, s, _ = qkv.shape
    nk = s // tq
    col = lambda off: pl.BlockSpec((1, s, HEAD_DIM), lambda bi, h: (bi, 0, off + h))
    return pl.pallas_call(
        functools.partial(_attn_kernel, tq=tq),
        grid=(b, n_heads),
        in_specs=[col(0), col(n_heads), col(2 * n_heads),
                  pl.BlockSpec((LANES, HEAD_DIM), lambda bi, h: (0, n_heads + h)),
                  pl.BlockSpec((LANES, HEAD_DIM), lambda bi, h: (0, 2 * n_heads + h)),
                  pl.BlockSpec((1, 1, nk, tq), lambda bi, h: (bi, h, 0, 0)),
                  pl.BlockSpec((1, 1, LANES), lambda bi, h: (h, 0, 0))],
        out_specs=col(0),
        out_shape=jax.ShapeDtypeStruct((b, s, n_heads * HEAD_DIM), BF16),
        compiler_params=_params("parallel", "parallel"),
        name="fox_attention",
    )(qkv, qkv, qkv, kvm, kvm, cx, cm)


def _conv_kernel(ua_ref, ug_ref, ma_ref, mg_ref, w_ref, cb_ref, g_ref, b_ref, o_ref, hb, hs, cbuf, *, ts):
    t = pl.program_id(1)
    nch = hb.shape[0]
    off = CONV_HALO - (CONV_KERNEL - 1)
    span = ts + CONV_HALO - SUBLANES
    lanes = lambda j: slice(j * LANES, (j + 1) * LANES)

    @pl.when(t == 0)
    def _():
        for j in range(nch):
            hb[j, 0:CONV_HALO - N_META, :] = jnp.zeros((CONV_HALO - N_META, LANES), F32)
            hb[j, CONV_HALO - N_META:CONV_HALO, :] = (
                ma_ref[:, lanes(j)].astype(F32) * jax.nn.sigmoid(mg_ref[:, lanes(j)].astype(F32)))

    for j in range(nch):
        hb[j, CONV_HALO:CONV_HALO + ts, :] = (
            ua_ref[0, :, lanes(j)].astype(F32) * jax.nn.sigmoid(ug_ref[0, :, lanes(j)].astype(F32)))

    def lane_chunk(j, carry):
        for s in range(1, SUBLANES):
            hs[s - 1] = hb[j, pl.ds(s, span), :]

        def row_chunk(i, c2):
            r0 = pl.multiple_of(i * CONV_ROWS, CONV_ROWS)
            tile_rows = lambda v: jnp.concatenate([v] * (CONV_ROWS // SUBLANES), axis=0)
            acc = tile_rows(cb_ref[j])
            for k in range(CONV_KERNEL):
                a, s = divmod(off + k, SUBLANES)
                rows = pl.ds(r0 + SUBLANES * a, CONV_ROWS)
                src = hb[j, rows, :] if s == 0 else hs[s - 1, rows, :]
                acc = acc + tile_rows(w_ref[j, k]) * src
            cbuf[j, pl.ds(r0, CONV_ROWS), :] = acc
            return c2

        lax.fori_loop(0, ts // CONV_ROWS, row_chunk, 0)
        return carry

    lax.fori_loop(0, nch, lane_chunk, 0)

    for j in range(nch):
        hb[j, 0:CONV_HALO, :] = hb[j, ts:ts + CONV_HALO, :]

    c = cbuf[...]
    n_ch = nch * LANES
    mu = jnp.sum(jnp.sum(c, axis=0), axis=-1, keepdims=True) / n_ch
    xc = c - mu[None]
    var = jnp.sum(jnp.sum(xc * xc, axis=0), axis=-1, keepdims=True) / n_ch
    y = xc * lax.rsqrt(var + LN_EPS)[None] * g_ref[...] + b_ref[...]
    y = y * jax.nn.sigmoid(y)
    for j in range(nch):
        o_ref[0, :, lanes(j)] = y[j].astype(o_ref.dtype)


def _conv_branch(u, um, conv_w, conv_b, ln_g, ln_b, ts):
    b, s, c2 = u.shape
    c = c2 // 2
    nch = c // LANES
    chunked = lambda v: jnp.transpose(v.astype(F32).reshape(-1, nch, LANES), (1, 0, 2))
    w_rep = jnp.broadcast_to(chunked(conv_w)[:, :, None, :], (nch, CONV_KERNEL, SUBLANES, LANES))
    cb_rep = jnp.broadcast_to(chunked(conv_b), (nch, SUBLANES, LANES))
    full = lambda shape: pl.BlockSpec(shape, lambda bi, t: (0,) * len(shape))
    return pl.pallas_call(
        functools.partial(_conv_kernel, ts=ts),
        grid=(b, s // ts),
        in_specs=[pl.BlockSpec((1, ts, c), lambda bi, t: (bi, t, 0)),
                  pl.BlockSpec((1, ts, c), lambda bi, t: (bi, t, 1)),
                  pl.BlockSpec((N_META, c), lambda bi, t: (0, 0)),
                  pl.BlockSpec((N_META, c), lambda bi, t: (0, 1)),
                  full((nch, CONV_KERNEL, SUBLANES, LANES)),
                  full((nch, SUBLANES, LANES)),
                  full((nch, 1, LANES)),
                  full((nch, 1, LANES))],
        out_specs=pl.BlockSpec((1, ts, c), lambda bi, t: (bi, t, 0)),
        out_shape=jax.ShapeDtypeStruct((b, s, c), BF16),
        scratch_shapes=[pltpu.VMEM((nch, ts + CONV_HALO, LANES), F32),
                        pltpu.VMEM((SUBLANES - 1, ts + CONV_HALO - SUBLANES, LANES), F32),
                        pltpu.VMEM((nch, ts, LANES), F32)],
        compiler_params=_params("parallel", "arbitrary"),
        name="conformer_conv",
    )(u, u, um, um, w_rep, cb_rep, chunked(ln_g), chunked(ln_b))


def _outproj_kernel(a_ref, c_ref, wa_ref, wc_ref, x_ref, o_ref, wab_ref, wcb_ref):
    @pl.when(pl.program_id(1) == 0)
    def _():
        wab_ref[...] = wa_ref[0].astype(BF16)
        wcb_ref[...] = wc_ref[0].astype(BF16)

    acc = jnp.dot(a_ref[...], wab_ref[...], preferred_element_type=F32)
    acc = acc + jnp.dot(c_ref[...], wcb_ref[...], preferred_element_type=F32)
    o_ref[...] = x_ref[...] + acc


def _out_projection(a, c, w, x, tm, tn):
    m, ka = a.shape
    kc = c.shape[1]
    n = w.shape[2]
    assert ka == kc
    return pl.pallas_call(
        _outproj_kernel,
        grid=(n // tn, m // tm),
        in_specs=[pl.BlockSpec((tm, ka), lambda j, i: (i, 0)),
                  pl.BlockSpec((tm, kc), lambda j, i: (i, 0)),
                  pl.BlockSpec((1, ka, tn), lambda j, i: (0, 0, j)),
                  pl.BlockSpec((1, kc, tn), lambda j, i: (0, 1, j)),
                  pl.BlockSpec((tm, tn), lambda j, i: (i, j))],
        out_specs=pl.BlockSpec((tm, tn), lambda j, i: (i, j)),
        out_shape=jax.ShapeDtypeStruct((m, n), F32),
        scratch_shapes=[pltpu.VMEM((ka, tn), BF16), pltpu.VMEM((kc, tn), BF16)],
        compiler_params=_params("parallel", "arbitrary"),
        name="out_projection",
    )(a, c, w, w, x)


def _router_kernel(h_ref, g_ref, wr_ref, br_ref, xp_ref, info_ref):
    h = h_ref[...]
    xn = h * lax.rsqrt(jnp.mean(h * h, axis=-1, keepdims=True) + RMS_EPS) * g_ref[...]
    half = xn.shape[1] // 2
    xp_ref[...] = _pack_bf16_pair(xn[:, :half], xn[:, half:])
    logits = jnp.dot(xn, wr_ref[...], precision=lax.Precision.HIGHEST, preferred_element_type=F32) + br_ref[...]
    lane = lax.broadcasted_iota(jnp.int32, logits.shape, 1).astype(F32)
    rmax = lambda v: jnp.max(v, axis=-1, keepdims=True)
    rmin = lambda v: jnp.min(v, axis=-1, keepdims=True)
    rsum = lambda v: jnp.sum(v, axis=-1, keepdims=True)
    far = float(LANES)

    is_g = lane < N_GROUPS
    lg = jnp.where(is_g, logits, -jnp.inf)
    mg = rmax(lg)
    g_idx = rmin(jnp.where(lg == mg, lane, far))
    g_w = 1.0 / rsum(jnp.where(is_g, jnp.exp(lg - mg), 0.0))

    lo = N_GROUPS + EXPERTS_PER_GROUP * g_idx
    in_e = jnp.logical_and(lane >= lo, lane < lo + EXPERTS_PER_GROUP)
    le = jnp.where(in_e, logits, -jnp.inf)
    pe = jnp.where(in_e, jnp.exp(le - rmax(le)), 0.0)
    prob = jnp.where(in_e, pe / rsum(pe), -1.0)
    p1 = rmax(prob)
    i1 = rmin(jnp.where(prob == p1, lane, far))
    prob2 = jnp.where(lane == i1, -1.0, prob)
    p2 = rmax(prob2)
    i2 = rmin(jnp.where(prob2 == p2, lane, far))
    den = p1 + p2
    w1 = g_w * (p1 / den)
    w2 = g_w * (p2 / den)
    info = jnp.where(lane == 0, w1, jnp.where(lane == 1, w2, jnp.where(
        lane == 2, i1 - N_GROUPS, jnp.where(lane == 3, i2 - N_GROUPS, 0.0))))
    info_ref[...] = info


def _norm_router(h, g, wr, br, tm):
    m, d = h.shape
    return pl.pallas_call(
        _router_kernel,
        grid=(m // tm,),
        in_specs=[pl.BlockSpec((tm, d), lambda i: (i, 0)),
                  pl.BlockSpec((1, d), lambda i: (0, 0)),
                  pl.BlockSpec((d, LANES), lambda i: (0, 0)),
                  pl.BlockSpec((1, LANES), lambda i: (0, 0))],
        out_specs=[pl.BlockSpec((tm, d // 2), lambda i: (i, 0)), pl.BlockSpec((tm, LANES), lambda i: (i, 0))],
        out_shape=[jax.ShapeDtypeStruct((m, d // 2), U32), jax.ShapeDtypeStruct((m, LANES), F32)],
        compiler_params=_params("parallel"),
        name="norm_router",
    )(h, g.reshape(1, d).astype(F32), wr, br)


def _moe_kernel(te_ref, tn_ref, tb_ref, asg_ref, xp_hbm, wg_ref, wu_ref, wd_ref, y_hbm,
                xq, xb, acc, ystage, pend, gsem, ssem, *, t_rows):
    t = pl.program_id(0)
    f = pl.program_id(1)
    n_tiles = pl.num_programs(0)
    nf = pl.num_programs(1)
    n = tn_ref[t]
    tm, half = ystage.shape
    groups = lambda cnt: (cnt + DMA_GROUP - 1) // DMA_GROUP

    def gather_copy(r, tok):
        return pltpu.make_async_copy(xp_hbm.at[pl.ds(tok, 1), :], xq.at[pl.ds(r, 1), :], gsem)

    def scatter_copy(r, row):
        return pltpu.make_async_copy(ystage.at[pl.ds(r, 1), :], y_hbm.at[pl.ds(row, 1), :], ssem)

    def start_gather(tile):
        base = tb_ref[tile]
        cnt = tn_ref[tile]

        def body(g, carry):
            for u in range(DMA_GROUP):
                r = g * DMA_GROUP + u
                gather_copy(r, asg_ref[base + jnp.minimum(r, cnt - 1)] >> 1).start()
            return carry
        lax.fori_loop(0, groups(cnt), body, 0)

    def wait_gather(cnt):
        def body(g, carry):
            for u in range(DMA_GROUP):
                gather_copy(g * DMA_GROUP + u, 0).wait()
            return carry
        lax.fori_loop(0, groups(cnt), body, 0)

    def start_scatter():
        base = tb_ref[t]

        def body(g, carry):
            for u in range(DMA_GROUP):
                r = g * DMA_GROUP + u
                a = asg_ref[base + jnp.minimum(r, n - 1)]
                row = jnp.where(r < n, (a & 1) * t_rows + (a >> 1), 2 * t_rows + u)
                scatter_copy(r, row).start()
            return carry
        lax.fori_loop(0, groups(n), body, 0)
        pend[0] = groups(n)

    def wait_scatter():
        def body(g, carry):
            for u in range(DMA_GROUP):
                scatter_copy(g * DMA_GROUP + u, 0).wait()
            return carry
        lax.fori_loop(0, pend[0], body, 0)
        pend[0] = 0

    @pl.when(jnp.logical_and(t == 0, f == 0))
    def _():
        xq[...] = jnp.zeros(xq.shape, xq.dtype)
        pend[0] = 0
        spare = pltpu.make_async_copy(xq.at[pl.ds(0, DMA_GROUP), :], y_hbm.at[pl.ds(2 * t_rows, DMA_GROUP), :], ssem)
        spare.start()
        spare.wait()
        start_gather(0)

    n_granules = (n + MOE_ROW_GRANULE - 1) // MOE_ROW_GRANULE
    row_counts = [k * MOE_ROW_GRANULE for k in range(1, tm // MOE_ROW_GRANULE + 1)]

    def unpack_rows(m):
        lo, hi = _unpack_bf16_pair(xq[0:m, :])
        xb[0:m, :half] = lo.astype(BF16)
        xb[0:m, half:] = hi.astype(BF16)

    @pl.when(jnp.logical_and(f == 0, n > 0))
    def _():
        wait_gather(n)
        for m in row_counts:
            pl.when(n_granules * MOE_ROW_GRANULE == m)(functools.partial(unpack_rows, m))

    @pl.when(jnp.logical_and(f == 0, t + 1 < n_tiles))
    def _():
        start_gather(t + 1)

    def ffn_rows(m):
        x = xb[0:m, :]
        g = jnp.dot(x, wg_ref[0].astype(BF16), preferred_element_type=F32)
        u = jnp.dot(x, wu_ref[0].astype(BF16), preferred_element_type=F32)
        hid = ((g * jax.nn.sigmoid(g)) * u).astype(BF16)
        part = jnp.dot(hid, wd_ref[0].astype(BF16), preferred_element_type=F32)

        @pl.when(f == 0)
        def _():
            acc[0:m, :] = part

        @pl.when(f > 0)
        def _():
            acc[0:m, :] = acc[0:m, :] + part

        @pl.when(f == nf - 1)
        def _():
            wait_scatter()
            ystage[0:m, :] = _pack_bf16_pair(acc[0:m, :half], acc[0:m, half:])
            start_scatter()

    for m in row_counts:
        pl.when(n_granules * MOE_ROW_GRANULE == m)(functools.partial(ffn_rows, m))

    @pl.when(jnp.logical_and(f == nf - 1, t == n_tiles - 1))
    def _():
        wait_scatter()


def _moe(xp, tile_e, tile_n, tile_base, asg, w_gate, w_up, w_down, tm, fc):
    assert tm % MOE_ROW_GRANULE == 0 and MOE_ROW_GRANULE % DMA_GROUP == 0
    t_rows, half = xp.shape
    d = 2 * half
    n_tiles = tile_e.shape[0]
    ff = w_gate.shape[3]
    nf = ff // fc

    def fidx(f, tn, t):
        return jnp.where(tn[t] > 0, f, nf - 1)

    grid_spec = pltpu.PrefetchScalarGridSpec(
        num_scalar_prefetch=4,
        grid=(n_tiles, nf),
        in_specs=[pl.BlockSpec(memory_space=pl.ANY),
                  pl.BlockSpec((1, 1, d, fc), lambda t, f, te, tn, tb, ds: (0, te[t], 0, fidx(f, tn, t))),
                  pl.BlockSpec((1, 1, d, fc), lambda t, f, te, tn, tb, ds: (0, te[t], 0, fidx(f, tn, t))),
                  pl.BlockSpec((1, 1, fc, d), lambda t, f, te, tn, tb, ds: (0, te[t], fidx(f, tn, t), 0))],
        out_specs=pl.BlockSpec(memory_space=pl.ANY),
        scratch_shapes=[pltpu.VMEM((tm, half), U32), pltpu.VMEM((tm, d), BF16), pltpu.VMEM((tm, d), F32),
                        pltpu.VMEM((tm, half), U32), pltpu.SMEM((1,), jnp.int32),
                        pltpu.SemaphoreType.DMA(()), pltpu.SemaphoreType.DMA(())],
    )
    return pl.pallas_call(
        functools.partial(_moe_kernel_4d, t_rows=t_rows),
        grid_spec=grid_spec,
        out_shape=jax.ShapeDtypeStruct((2 * t_rows + DMA_GROUP, half), U32),
        compiler_params=_params("arbitrary", "arbitrary"),
        name="sparse_moe",
    )(tile_e, tile_n, tile_base, asg, xp, w_gate, w_up, w_down)


def _moe_kernel_4d(te_ref, tn_ref, tb_ref, asg_ref, xp_hbm, wg_ref, wu_ref, wd_ref, *rest, t_rows):
    _moe_kernel(te_ref, tn_ref, tb_ref, asg_ref, xp_hbm, wg_ref.at[0], wu_ref.at[0], wd_ref.at[0], *rest,
                t_rows=t_rows)


def _routing_tables(e_idx, tm, n_tiles):
    flat_e = e_idx.reshape(-1)
    n_assign = flat_e.shape[0]
    onehot = (flat_e[:, None] == jnp.arange(N_EXPERTS, dtype=jnp.int32)[None, :]).astype(jnp.int32)
    csum = jnp.cumsum(onehot, axis=0)
    rank = jnp.sum(csum * onehot, axis=1) - 1
    counts = csum[-1]
    row_end = jnp.cumsum(counts)
    row_start = row_end - counts
    tiles_per_e = (counts + tm - 1) // tm
    tile_end = jnp.cumsum(tiles_per_e)
    tile_start = tile_end - tiles_per_e
    total = tile_end[-1]
    pos = jnp.sum(onehot * row_start[None, :], axis=1) + rank
    asg = jnp.zeros((n_assign,), jnp.int32).at[pos].set(jnp.arange(n_assign, dtype=jnp.int32), unique_indices=True)
    tid = jnp.arange(n_tiles, dtype=jnp.int32)
    te = jnp.sum((tid[:, None] >= tile_end[None, :]).astype(jnp.int32), axis=1)
    te = jnp.minimum(te, N_EXPERTS - 1)
    active = tid < total
    first = (tid - tile_start[te]) * tm
    tn = jnp.where(active, jnp.clip(counts[te] - first, 0, tm), 0).astype(jnp.int32)
    tb = jnp.where(active, row_start[te] + first, 0).astype(jnp.int32)
    last_e = te[jnp.maximum(total - 1, 0)]
    te = jnp.where(active, te, last_e).astype(jnp.int32)
    return te, tn, tb, asg


def _final_kernel(h_ref, y0_ref, y1_ref, info_ref, g_ref, o_ref):
    w = info_ref[...]
    w0 = w[:, 0:1]
    w1 = w[:, 1:2]
    half = y0_ref.shape[1]
    y0_lo, y0_hi = _unpack_bf16_pair(y0_ref[...])
    y1_lo, y1_hi = _unpack_bf16_pair(y1_ref[...])
    h_lo = h_ref[:, :half] + w0 * y0_lo + w1 * y1_lo
    h_hi = h_ref[:, half:] + w0 * y0_hi + w1 * y1_hi
    ms = (jnp.sum(h_lo * h_lo, axis=-1, keepdims=True) + jnp.sum(h_hi * h_hi, axis=-1, keepdims=True)) / (2 * half)
    r = lax.rsqrt(ms + RMS_EPS)
    o_ref[:, :half] = h_lo * r * g_ref[:, :half]
    o_ref[:, half:] = h_hi * r * g_ref[:, half:]


def _combine_final(h, y, info, g, tm):
    m, d = h.shape
    return pl.pallas_call(
        _final_kernel,
        grid=(m // tm,),
        in_specs=[pl.BlockSpec((tm, d), lambda i: (i, 0)),
                  pl.BlockSpec((tm, d // 2), lambda i: (i, 0)),
                  pl.BlockSpec((tm, d // 2), lambda i: (m // tm + i, 0)),
                  pl.BlockSpec((tm, LANES), lambda i: (i, 0)),
                  pl.BlockSpec((1, d), lambda i: (0, 0))],
        out_specs=pl.BlockSpec((tm, d), lambda i: (i, 0)),
        out_shape=jax.ShapeDtypeStruct((m, d), F32),
        compiler_params=_params("parallel"),
        name="combine_final",
    )(h, y, y, info, g.reshape(1, d).astype(F32))


def kernel(x, meta_tokens, attn_norm_g, w_in, forget_bias, conv_w, conv_b, conv_ln_g, conv_ln_b, w_out,
           ffn_norm_g, w_router_group, b_router_group, w_router_expert, b_router_expert,
           w_expert_gate, w_expert_up, w_expert_down, final_norm_g):
    b, s, d = x.shape
    depth = w_in.shape[0]
    assert depth == 1, "one layer: the meta rows are not carried past the mixer"
    n_heads = forget_bias.shape[1]
    aw = n_heads * HEAD_DIM
    cw = conv_w.shape[2]
    t_rows = b * s
    x2 = x.reshape(t_rows, d)

    xn = _rmsnorm(x2, attn_norm_g[0], BF16, tm=512)
    mn = _rmsnorm(meta_tokens.astype(x.dtype), attn_norm_g[0], BF16, tm=N_META)

    w_t = jnp.transpose(w_in[0])
    qkv, qkv_m = _projection(xn, mn, w_t, 0, 3 * aw, BF16, 1024, 512, "in_proj_qkv")
    fx, f_m = _projection(xn, mn, w_t, 3 * aw, LANES, F32, 1024, LANES, "in_proj_gate")
    u, u_m = _projection(xn, mn, w_t, 3 * aw + n_heads, 2 * cw, BF16, 1024, 512, "in_proj_glu")

    tq = 512
    bias_f = jnp.zeros((1, LANES), F32).at[0, :n_heads].set(forget_bias[0].astype(F32))
    cx, cm = _forget_cumsum(fx.reshape(b, s, LANES), f_m, bias_f)
    cx = jnp.transpose(cx[:, :, :n_heads], (0, 2, 1)).reshape(b, n_heads, s // tq, tq)
    cm = jnp.zeros((n_heads, 1, LANES), F32).at[:, 0, :N_META].set(jnp.transpose(cm[0, :, :n_heads]))
    kvm = jnp.zeros((LANES, 3 * aw), BF16).at[:N_META].set(qkv_m)
    attn = _attention(qkv.reshape(b, s, 3 * aw), kvm, cx, cm, n_heads, tq)

    conv = _conv_branch(u.reshape(b, s, 2 * cw), u_m, conv_w[0], conv_b[0], conv_ln_g[0], conv_ln_b[0], ts=256)

    h1 = _out_projection(attn.reshape(t_rows, aw), conv.reshape(t_rows, cw), w_out, x2, 1024, 512)

    n_r = N_GROUPS + N_EXPERTS
    wr = jnp.zeros((d, LANES), F32).at[:, :N_GROUPS].set(w_router_group[0]).at[:, N_GROUPS:n_r].set(w_router_expert[0])
    br = jnp.zeros((1, LANES), F32).at[0, :N_GROUPS].set(b_router_group[0]).at[0, N_GROUPS:n_r].set(b_router_expert[0])
    xp, info = _norm_router(h1, ffn_norm_g[0], wr, br, tm=256)

    tm_moe = 512
    n_tiles = (2 * t_rows) // tm_moe + N_EXPERTS
    te, tn, tb, asg = _routing_tables(info[:, 2:4].astype(jnp.int32), tm_moe, n_tiles)
    y = _moe(xp, te, tn, tb, asg, w_expert_gate, w_expert_up, w_expert_down, tm_moe, fc=256)

    out = _combine_final(h1, y, info, final_norm_g, tm=256)
    return out.reshape(b, s, d)
```

```python
import functools

import jax
import jax.numpy as jnp
from jax import lax
from jax.experimental import pallas as pl
from jax.experimental.pallas import tpu as pltpu

F32 = jnp.float32
BF16 = jnp.bfloat16
U32 = jnp.uint32

N_META = 16
HEAD_DIM = 128
CONV_KERNEL = 31
N_GROUPS = 8
EXPERTS_PER_GROUP = 8
N_EXPERTS = N_GROUPS * EXPERTS_PER_GROUP
RMS_EPS = 1e-6
LN_EPS = 1e-5

LANES = 128
SUBLANES = 8
VMEM_LIMIT = 56 * 1024 * 1024
CONV_HALO = 32
CONV_ROWS = 64
MOE_ROW_GRANULE = 64
DMA_GROUP = 8


def _params(*sem):
    return pltpu.CompilerParams(dimension_semantics=sem, vmem_limit_bytes=VMEM_LIMIT)


def _pack_bf16_pair(lo, hi):
    lo_bits = lax.bitcast_convert_type(lo.astype(BF16).astype(F32), U32)
    hi_bits = lax.bitcast_convert_type(hi.astype(BF16).astype(F32), U32)
    return (lo_bits >> 16) | hi_bits


def _unpack_bf16_pair(p):
    lo = lax.bitcast_convert_type(p << 16, F32)
    hi = lax.bitcast_convert_type(p & jnp.uint32(0xFFFF0000), F32)
    return lo, hi


def _rmsnorm_kernel(x_ref, g_ref, o_ref):
    x = x_ref[...].astype(F32)
    y = x * lax.rsqrt(jnp.mean(x * x, axis=-1, keepdims=True) + RMS_EPS)
    o_ref[...] = (y * g_ref[...]).astype(o_ref.dtype)


def _rmsnorm(x, g, out_dtype, tm):
    m, d = x.shape
    return pl.pallas_call(
        _rmsnorm_kernel,
        grid=(m // tm,),
        in_specs=[pl.BlockSpec((tm, d), lambda i: (i, 0)), pl.BlockSpec((1, d), lambda i: (0, 0))],
        out_specs=pl.BlockSpec((tm, d), lambda i: (i, 0)),
        out_shape=jax.ShapeDtypeStruct((m, d), out_dtype),
        compiler_params=_params("parallel"),
        name="rmsnorm",
    )(x, g.reshape(1, d).astype(F32))


def _proj_kernel(a_ref, am_ref, wt_ref, o_ref, om_ref, wb_ref):
    i = pl.program_id(1)
    tn, k = wt_ref.shape
    kc = min(k, 512)

    @pl.when(i == 0)
    def _():
        for c in range(k // kc):
            wb_ref[c * kc:(c + 1) * kc, :] = wt_ref[:, c * kc:(c + 1) * kc].T.astype(BF16)
        om_ref[...] = jnp.dot(am_ref[...], wb_ref[...], preferred_element_type=F32).astype(om_ref.dtype)

    o_ref[...] = jnp.dot(a_ref[...], wb_ref[...], preferred_element_type=F32).astype(o_ref.dtype)


def _projection(a, am, wt, row0, n_cols, out_dtype, tm, tn, name):
    m, k = a.shape
    mm = am.shape[0]
    assert n_cols % tn == 0 and m % tm == 0 and row0 % SUBLANES == 0
    if row0 % tn == 0:
        w_spec = pl.BlockSpec((tn, k), lambda j, i: (row0 // tn + j, 0))
    else:
        w_spec = pl.BlockSpec((pl.Element(tn), pl.Element(k)), lambda j, i: ((row0 // SUBLANES + j * (tn // SUBLANES)) * SUBLANES, 0))
    return pl.pallas_call(
        _proj_kernel,
        grid=(n_cols // tn, m // tm),
        in_specs=[pl.BlockSpec((tm, k), lambda j, i: (i, 0)),
                  pl.BlockSpec((mm, k), lambda j, i: (0, 0)),
                  w_spec],
        out_specs=[pl.BlockSpec((tm, tn), lambda j, i: (i, j)), pl.BlockSpec((mm, tn), lambda j, i: (0, j))],
        out_shape=[jax.ShapeDtypeStruct((m, n_cols), out_dtype), jax.ShapeDtypeStruct((mm, n_cols), out_dtype)],
        scratch_shapes=[pltpu.VMEM((k, tn), BF16)],
        compiler_params=_params("parallel", "arbitrary"),
        name=name,
    )(a, am, wt)


def _log_sigmoid(z):
    return jnp.minimum(z, 0.0) - jnp.log1p(jnp.exp(-jnp.abs(z)))


def _split3_bf16(x):
    hi = x.astype(BF16)
    r1 = x - hi.astype(F32)
    mid = r1.astype(BF16)
    lo = (r1 - mid.astype(F32)).astype(BF16)
    return hi, mid, lo


def _cumsum_rows(l, tril):
    dot = functools.partial(jnp.dot, preferred_element_type=F32)
    hi, mid, lo = _split3_bf16(l)
    return dot(tril, hi) + dot(tril, mid) + dot(tril, lo)


def _tril(n):
    r = lax.broadcasted_iota(jnp.int32, (n, n), 0)
    c = lax.broadcasted_iota(jnp.int32, (n, n), 1)
    return jnp.where(r >= c, 1.0, 0.0).astype(BF16)


def _cumsum_kernel(fx_ref, fm_ref, b_ref, cx_ref, cm_ref, *, blk):
    bias = b_ref[...]
    cm = _cumsum_rows(_log_sigmoid(fm_ref[...] + bias), _tril(N_META))
    cm_ref[0] = cm
    carry = cm[N_META - 1:N_META, :]
    tril = _tril(blk)
    for j in range(fx_ref.shape[1] // blk):
        l = _log_sigmoid(fx_ref[0, j * blk:(j + 1) * blk, :] + bias)
        c = _cumsum_rows(l, tril) + carry
        cx_ref[0, j * blk:(j + 1) * blk, :] = c
        carry = c[blk - 1:blk, :]


def _forget_cumsum(fx, fm, bias):
    b, s, _ = fx.shape
    return pl.pallas_call(
        functools.partial(_cumsum_kernel, blk=256),
        grid=(b,),
        in_specs=[pl.BlockSpec((1, s, LANES), lambda i: (i, 0, 0)),
                  pl.BlockSpec((N_META, LANES), lambda i: (0, 0)),
                  pl.BlockSpec((1, LANES), lambda i: (0, 0))],
        out_specs=[pl.BlockSpec((1, s, LANES), lambda i: (i, 0, 0)),
                   pl.BlockSpec((1, N_META, LANES), lambda i: (i, 0, 0))],
        out_shape=[jax.ShapeDtypeStruct((b, s, LANES), F32), jax.ShapeDtypeStruct((b, N_META, LANES), F32)],
        compiler_params=_params("parallel"),
        name="forget_cumsum",
    )(fx, fm, bias)


def _attn_kernel(q_ref, k_ref, v_ref, km_ref, vm_ref, cx_ref, cm_ref, o_ref, *, tq):
    scale = HEAD_DIM ** -0.5
    hq = tq // 2

    def scores(q, kb, cb):
        s = lax.dot_general(q, kb, (((1,), (1,)), ((), ())), preferred_element_type=F32)
        return s * scale - cb

    def update(s, vb, m, l, acc):
        m_new = jnp.maximum(m, jnp.max(s, axis=-1, keepdims=True))
        alpha = jnp.exp(m - m_new)
        p = jnp.exp(s - m_new)
        l = alpha * l + jnp.sum(p, axis=-1, keepdims=True)
        acc = alpha * acc + jnp.dot(p.astype(BF16), vb, preferred_element_type=F32)
        return m_new, l, acc

    def q_tile(i, carry):
        q0 = pl.multiple_of(i * tq, tq)
        q = q_ref[0, pl.ds(q0, tq), :]

        s = scores(q, km_ref[...], cm_ref[0])
        col = lax.broadcasted_iota(jnp.int32, s.shape, 1)
        s = jnp.where(col < N_META, s, -jnp.inf)
        m = jnp.max(s, axis=-1, keepdims=True)
        p = jnp.exp(s - m)
        l = jnp.sum(p, axis=-1, keepdims=True)
        acc = jnp.dot(p.astype(BF16), vm_ref[...], preferred_element_type=F32)

        def body(j, st):
            off = pl.multiple_of(j * tq, tq)
            s = scores(q, k_ref[0, pl.ds(off, tq), :], cx_ref[0, 0, pl.ds(j, 1), :])
            return update(s, v_ref[0, pl.ds(off, tq), :], *st)

        m, l, acc = lax.fori_loop(0, i, body, (m, l, acc))

        cb = cx_ref[0, 0, pl.ds(i, 1), :]
        s = scores(q[:hq], k_ref[0, pl.ds(q0, hq), :], cb[:, :hq])
        row = lax.broadcasted_iota(jnp.int32, s.shape, 0)
        col = lax.broadcasted_iota(jnp.int32, s.shape, 1)
        top = update(jnp.where(row >= col, s, -jnp.inf), v_ref[0, pl.ds(q0, hq), :], m[:hq], l[:hq], acc[:hq])
        s = scores(q[hq:], k_ref[0, pl.ds(q0, tq), :], cb)
        row = lax.broadcasted_iota(jnp.int32, s.shape, 0) + hq
        col = lax.broadcasted_iota(jnp.int32, s.shape, 1)
        bot = update(jnp.where(row >= col, s, -jnp.inf), v_ref[0, pl.ds(q0, tq), :], m[hq:], l[hq:], acc[hq:])
        o_ref[0, pl.ds(q0, hq), :] = (top[2] / top[1]).astype(o_ref.dtype)
        o_ref[0, pl.ds(q0 + hq, hq), :] = (bot[2] / bot[1]).astype(o_ref.dtype)
        return carry

    lax.fori_loop(0, q_ref.shape[1] // tq, q_tile, 0)


def _attention(qkv, kvm, cx, cm, n_heads, tq):
    b, s, _ = qkv.shape
    nk = s // tq
    col = lambda off: pl.BlockSpec((1, s, HEAD_DIM), lambda bi, h: (bi, 0, off + h))
    return pl.pallas_call(
        functools.partial(_attn_kernel, tq=tq),
        grid=(b, n_heads),
        in_specs=[col(0), col(n_heads), col(2 * n_heads),
                  pl.BlockSpec((LANES, HEAD_DIM), lambda bi, h: (0, n_heads + h)),
                  pl.BlockSpec((LANES, HEAD_DIM), lambda bi, h: (0, 2 * n_heads + h)),
                  pl.BlockSpec((1, 1, nk, tq), lambda bi, h: (bi, h, 0, 0)),
                  pl.BlockSpec((1, 1, LANES), lambda bi, h: (h, 0, 0))],
        out_specs=col(0),
        out_shape=jax.ShapeDtypeStruct((b, s, n_heads * HEAD_DIM), BF16),
        compiler_params=_params("parallel", "parallel"),
        name="fox_attention",
    )(qkv, qkv, qkv, kvm, kvm, cx, cm)


def _conv_kernel(ua_ref, ug_ref, ma_ref, mg_ref, w_ref, cb_ref, g_ref, b_ref, o_ref, hb, hs, cbuf, *, ts):
    t = pl.program_id(1)
    nch = hb.shape[0]
    off = CONV_HALO - (CONV_KERNEL - 1)
    span = ts + CONV_HALO - SUBLANES
    lanes = lambda j: slice(j * LANES, (j + 1) * LANES)

    @pl.when(t == 0)
    def _():
        for j in range(nch):
            hb[j, 0:CONV_HALO - N_META, :] = jnp.zeros((CONV_HALO - N_META, LANES), F32)
            hb[j, CONV_HALO - N_META:CONV_HALO, :] = (
                ma_ref[:, lanes(j)].astype(F32) * jax.nn.sigmoid(mg_ref[:, lanes(j)].astype(F32)))

    for j in range(nch):
        hb[j, CONV_HALO:CONV_HALO + ts, :] = (
            ua_ref[0, :, lanes(j)].astype(F32) * jax.nn.sigmoid(ug_ref[0, :, lanes(j)].astype(F32)))

    def lane_chunk(j, carry):
        for s in range(1, SUBLANES):
            hs[s - 1] = hb[j, pl.ds(s, span), :]

        def row_chunk(i, c2):
            r0 = pl.multiple_of(i * CONV_ROWS, CONV_ROWS)
            tile_rows = lambda v: jnp.concatenate([v] * (CONV_ROWS // SUBLANES), axis=0)
            acc = tile_rows(cb_ref[j])
            for k in range(CONV_KERNEL):
                a, s = divmod(off + k, SUBLANES)
                rows = pl.ds(r0 + SUBLANES * a, CONV_ROWS)
                src = hb[j, rows, :] if s == 0 else hs[s - 1, rows, :]
                acc = acc + tile_rows(w_ref[j, k]) * src
            cbuf[j, pl.ds(r0, CONV_ROWS), :] = acc
            return c2

        lax.fori_loop(0, ts // CONV_ROWS, row_chunk, 0)
        return carry

    lax.fori_loop(0, nch, lane_chunk, 0)

    for j in range(nch):
        hb[j, 0:CONV_HALO, :] = hb[j, ts:ts + CONV_HALO, :]

    c = cbuf[...]
    n_ch = nch * LANES
    mu = jnp.sum(jnp.sum(c, axis=0), axis=-1, keepdims=True) / n_ch
    xc = c - mu[None]
    var = jnp.sum(jnp.sum(xc * xc, axis=0), axis=-1, keepdims=True) / n_ch
    y = xc * lax.rsqrt(var + LN_EPS)[None] * g_ref[...] + b_ref[...]
    y = y * jax.nn.sigmoid(y)
    for j in range(nch):
        o_ref[0, :, lanes(j)] = y[j].astype(o_ref.dtype)


def _conv_branch(u, um, conv_w, conv_b, ln_g, ln_b, ts):
    b, s, c2 = u.shape
    c = c2 // 2
    nch = c // LANES
    chunked = lambda v: jnp.transpose(v.astype(F32).reshape(-1, nch, LANES), (1, 0, 2))
    w_rep = jnp.broadcast_to(chunked(conv_w)[:, :, None, :], (nch, CONV_KERNEL, SUBLANES, LANES))
    cb_rep = jnp.broadcast_to(chunked(conv_b), (nch, SUBLANES, LANES))
    full = lambda shape: pl.BlockSpec(shape, lambda bi, t: (0,) * len(shape))
    return pl.pallas_call(
        functools.partial(_conv_kernel, ts=ts),
        grid=(b, s // ts),
        in_specs=[pl.BlockSpec((1, ts, c), lambda bi, t: (bi, t, 0)),
                  pl.BlockSpec((1, ts, c), lambda bi, t: (bi, t, 1)),
                  pl.BlockSpec((N_META, c), lambda bi, t: (0, 0)),
                  pl.BlockSpec((N_META, c), lambda bi, t: (0, 1)),
                  full((nch, CONV_KERNEL, SUBLANES, LANES)),
                  full((nch, SUBLANES, LANES)),
                  full((nch, 1, LANES)),
                  full((nch, 1, LANES))],
        out_specs=pl.BlockSpec((1, ts, c), lambda bi, t: (bi, t, 0)),
        out_shape=jax.ShapeDtypeStruct((b, s, c), BF16),
        scratch_shapes=[pltpu.VMEM((nch, ts + CONV_HALO, LANES), F32),
                        pltpu.VMEM((SUBLANES - 1, ts + CONV_HALO - SUBLANES, LANES), F32),
                        pltpu.VMEM((nch, ts, LANES), F32)],
        compiler_params=_params("parallel", "arbitrary"),
        name="conformer_conv",
    )(u, u, um, um, w_rep, cb_rep, chunked(ln_g), chunked(ln_b))


def _outproj_kernel(a_ref, c_ref, wa_ref, wc_ref, x_ref, o_ref, wab_ref, wcb_ref):
    @pl.when(pl.program_id(1) == 0)
    def _():
        wab_ref[...] = wa_ref[0].astype(BF16)
        wcb_ref[...] = wc_ref[0].astype(BF16)

    acc = jnp.dot(a_ref[...], wab_ref[...], preferred_element_type=F32)
    acc = acc + jnp.dot(c_ref[...], wcb_ref[...], preferred_element_type=F32)
    o_ref[...] = x_ref[...] + acc


def _out_projection(a, c, w, x, tm, tn):
    m, ka = a.shape
    kc = c.shape[1]
    n = w.shape[2]
    assert ka == kc
    return pl.pallas_call(
        _outproj_kernel,
        grid=(n // tn, m // tm),
        in_specs=[pl.BlockSpec((tm, ka), lambda j, i: (i, 0)),
                  pl.BlockSpec((tm, kc), lambda j, i: (i, 0)),
                  pl.BlockSpec((1, ka, tn), lambda j, i: (0, 0, j)),
                  pl.BlockSpec((1, kc, tn), lambda j, i: (0, 1, j)),
                  pl.BlockSpec((tm, tn), lambda j, i: (i, j))],
        out_specs=pl.BlockSpec((tm, tn), lambda j, i: (i, j)),
        out_shape=jax.ShapeDtypeStruct((m, n), F32),
        scratch_shapes=[pltpu.VMEM((ka, tn), BF16), pltpu.VMEM((kc, tn), BF16)],
        compiler_params=_params("parallel", "arbitrary"),
        name="out_projection",
    )(a, c, w, w, x)


def _router_kernel(h_ref, g_ref, wr_ref, br_ref, xp_ref, info_ref):
    h = h_ref[...]
    xn = h * lax.rsqrt(jnp.mean(h * h, axis=-1, keepdims=True) + RMS_EPS) * g_ref[...]
    half = xn.shape[1] // 2
    xp_ref[...] = _pack_bf16_pair(xn[:, :half], xn[:, half:])
    logits = lax.dot_general(wr_ref[...], xn, (((1,), (1,)), ((), ())), precision=lax.Precision.HIGHEST,
                             preferred_element_type=F32) + br_ref[...]
    lane = lax.broadcasted_iota(jnp.int32, logits.shape, 0).astype(F32)
    rmax = lambda v: jnp.max(v, axis=0, keepdims=True)
    rmin = lambda v: jnp.min(v, axis=0, keepdims=True)
    rsum = lambda v: jnp.sum(v, axis=0, keepdims=True)
    far = float(LANES)

    is_g = lane < N_GROUPS
    lg = jnp.where(is_g, logits, -jnp.inf)
    mg = rmax(lg)
    g_idx = rmin(jnp.where(lg == mg, lane, far))
    g_w = 1.0 / rsum(jnp.where(is_g, jnp.exp(lg - mg), 0.0))

    lo = N_GROUPS + EXPERTS_PER_GROUP * g_idx
    in_e = jnp.logical_and(lane >= lo, lane < lo + EXPERTS_PER_GROUP)
    le = jnp.where(in_e, logits, -jnp.inf)
    pe = jnp.where(in_e, jnp.exp(le - rmax(le)), 0.0)
    prob = jnp.where(in_e, pe / rsum(pe), -1.0)
    p1 = rmax(prob)
    i1 = rmin(jnp.where(prob == p1, lane, far))
    prob2 = jnp.where(lane == i1, -1.0, prob)
    p2 = rmax(prob2)
    i2 = rmin(jnp.where(prob2 == p2, lane, far))
    den = p1 + p2
    w1 = g_w * (p1 / den)
    w2 = g_w * (p2 / den)
    zeros = jnp.zeros((SUBLANES - 4,) + w1.shape[1:], F32)
    info_ref[...] = jnp.concatenate([w1, w2, i1 - N_GROUPS, i2 - N_GROUPS, zeros], axis=0)


def _norm_router(h, g, wr_t, br_t, tm):
    m, d = h.shape
    return pl.pallas_call(
        _router_kernel,
        grid=(m // tm,),
        in_specs=[pl.BlockSpec((tm, d), lambda i: (i, 0)),
                  pl.BlockSpec((1, d), lambda i: (0, 0)),
                  pl.BlockSpec((LANES, d), lambda i: (0, 0)),
                  pl.BlockSpec((LANES, 1), lambda i: (0, 0))],
        out_specs=[pl.BlockSpec((tm, d // 2), lambda i: (i, 0)), pl.BlockSpec((SUBLANES, tm), lambda i: (0, i))],
        out_shape=[jax.ShapeDtypeStruct((m, d // 2), U32), jax.ShapeDtypeStruct((SUBLANES, m), F32)],
        compiler_params=_params("parallel"),
        name="norm_router",
    )(h, g.reshape(1, d).astype(F32), wr_t, br_t)


def _moe_kernel(te_ref, tn_ref, tb_ref, asg_ref, xp_hbm, wg_ref, wu_ref, wd_ref, y_hbm,
                xq, xb, gacc, uacc, wd0, ystage, pend, gsem, ssem, *, y_tail_rows):
    t = pl.program_id(0)
    f = pl.program_id(1)
    n_tiles = pl.num_programs(0)
    n = tn_ref[t]
    tm, half = ystage.shape
    groups = lambda cnt: (cnt + DMA_GROUP - 1) // DMA_GROUP

    def gather_copy(r, tok):
        return pltpu.make_async_copy(xp_hbm.at[pl.ds(tok, 1), :], xq.at[pl.ds(r, 1), :], gsem)

    def start_gather(tile):
        base = tb_ref[tile]
        cnt = tn_ref[tile]

        def body(g, carry):
            for u in range(DMA_GROUP):
                r = g * DMA_GROUP + u
                gather_copy(r, asg_ref[base + jnp.minimum(r, cnt - 1)] >> 1).start()
            return carry
        lax.fori_loop(0, groups(cnt), body, 0)

    def wait_gather(cnt):
        def body(g, carry):
            for u in range(DMA_GROUP):
                gather_copy(g * DMA_GROUP + u, 0).wait()
            return carry
        lax.fori_loop(0, groups(cnt), body, 0)

    n_granules = (n + MOE_ROW_GRANULE - 1) // MOE_ROW_GRANULE
    row_counts = [k * MOE_ROW_GRANULE for k in range(1, tm // MOE_ROW_GRANULE + 1)]

    def out_copy(m, base):
        return pltpu.make_async_copy(ystage.at[pl.ds(0, m), :], y_hbm.at[pl.ds(base, m), :], ssem)

    def start_scatter(m):
        out_copy(m, pl.multiple_of(tb_ref[t], SUBLANES)).start()
        pend[0] = m

    def wait_scatter():
        for m in row_counts:
            @pl.when(pend[0] == m)
            def _(m=m):
                out_copy(m, 0).wait()
        pend[0] = 0

    @pl.when(jnp.logical_and(t == 0, f == 0))
    def _():
        xq[...] = jnp.zeros(xq.shape, xq.dtype)
        pend[0] = 0
        for part in range(y_tail_rows // tm):
            fill = pltpu.make_async_copy(xq, y_hbm.at[pl.ds(y_hbm.shape[0] - (part + 1) * tm, tm), :], ssem)
            fill.start()
            fill.wait()
        start_gather(0)

    def unpack_rows(m):
        lo, hi = _unpack_bf16_pair(xq[0:m, :])
        xb[0, 0:m, :] = lo.astype(BF16)
        xb[1, 0:m, :] = hi.astype(BF16)

    @pl.when(jnp.logical_and(f == 0, n > 0))
    def _():
        wait_gather(n)
        for m in row_counts:
            pl.when(n_granules * MOE_ROW_GRANULE == m)(functools.partial(unpack_rows, m))
        wd0[...] = wd_ref[0].astype(BF16)

    @pl.when(jnp.logical_and(f == 0, t + 1 < n_tiles))
    def _():
        start_gather(t + 1)

    def ffn_rows(m):
        x = xb[f, 0:m, :]
        g_part = jnp.dot(x, wg_ref[0].astype(BF16), preferred_element_type=F32)
        u_part = jnp.dot(x, wu_ref[0].astype(BF16), preferred_element_type=F32)

        @pl.when(f == 0)
        def _():
            gacc[0:m, :] = g_part
            uacc[0:m, :] = u_part

        @pl.when(f == 1)
        def _():
            g = gacc[0:m, :] + g_part
            u = uacc[0:m, :] + u_part
            hid = ((g * jax.nn.sigmoid(g)) * u).astype(BF16)
            fh = wd0.shape[0]
            out = jnp.dot(hid[:, :fh], wd0[...], preferred_element_type=F32)
            out = out + jnp.dot(hid[:, fh:], wd_ref[0].astype(BF16), preferred_element_type=F32)
            wait_scatter()
            ystage[0:m, :] = _pack_bf16_pair(out[:, :half], out[:, half:])
            start_scatter(m)

    for m in row_counts:
        pl.when(n_granules * MOE_ROW_GRANULE == m)(functools.partial(ffn_rows, m))

    @pl.when(jnp.logical_and(f == 1, t == n_tiles - 1))
    def _():
        wait_scatter()


def _moe(xp, tile_e, tile_n, tile_base, asg, w_gate, w_up, w_down, tm):
    assert tm % MOE_ROW_GRANULE == 0 and MOE_ROW_GRANULE % DMA_GROUP == 0
    t_rows, half = xp.shape
    d = 2 * half
    n_tiles = tile_e.shape[0]
    ff = w_gate.shape[3]
    fh = ff // 2
    y_rows = asg.shape[0] + tm
    y_tail_rows = -(-(y_rows - 2 * t_rows) // tm) * tm

    def fidx(f, tn, t):
        return jnp.where(tn[t] > 0, f, 1)

    grid_spec = pltpu.PrefetchScalarGridSpec(
        num_scalar_prefetch=4,
        grid=(n_tiles, 2),
        in_specs=[pl.BlockSpec(memory_space=pl.ANY),
                  pl.BlockSpec((1, 1, half, ff), lambda t, f, te, tn, tb, ds: (0, te[t], fidx(f, tn, t), 0)),
                  pl.BlockSpec((1, 1, half, ff), lambda t, f, te, tn, tb, ds: (0, te[t], fidx(f, tn, t), 0)),
                  pl.BlockSpec((1, 1, fh, d), lambda t, f, te, tn, tb, ds: (0, te[t], fidx(f, tn, t), 0))],
        out_specs=pl.BlockSpec(memory_space=pl.ANY),
        scratch_shapes=[pltpu.VMEM((tm, half), U32), pltpu.VMEM((2, tm, half), BF16),
                        pltpu.VMEM((tm, ff), F32), pltpu.VMEM((tm, ff), F32), pltpu.VMEM((fh, d), BF16),
                        pltpu.VMEM((tm, half), U32), pltpu.SMEM((1,), jnp.int32),
                        pltpu.SemaphoreType.DMA(()), pltpu.SemaphoreType.DMA(())],
    )
    return pl.pallas_call(
        functools.partial(_moe_kernel_4d, y_tail_rows=y_tail_rows),
        grid_spec=grid_spec,
        out_shape=jax.ShapeDtypeStruct((y_rows, half), U32),
        compiler_params=_params("arbitrary", "arbitrary"),
        name="sparse_moe",
    )(tile_e, tile_n, tile_base, asg, xp, w_gate, w_up, w_down)


def _moe_kernel_4d(te_ref, tn_ref, tb_ref, asg_ref, xp_hbm, wg_ref, wu_ref, wd_ref, *rest, y_tail_rows):
    _moe_kernel(te_ref, tn_ref, tb_ref, asg_ref, xp_hbm, wg_ref.at[0], wu_ref.at[0], wd_ref.at[0], *rest,
                y_tail_rows=y_tail_rows)


def _routing_tables(e_idx, tm, n_tiles):
    flat_e = e_idx.reshape(-1)
    n_assign = flat_e.shape[0]
    onehot = (flat_e[:, None] == jnp.arange(N_EXPERTS, dtype=jnp.int32)[None, :]).astype(jnp.int32)
    csum = jnp.cumsum(onehot, axis=0)
    rank = jnp.sum(csum * onehot, axis=1) - 1
    counts = csum[-1]
    seg = (counts + SUBLANES - 1) // SUBLANES * SUBLANES
    row_end = jnp.cumsum(seg)
    row_start = row_end - seg
    tiles_per_e = (counts + tm - 1) // tm
    tile_end = jnp.cumsum(tiles_per_e)
    tile_start = tile_end - tiles_per_e
    total = tile_end[-1]
    pos = jnp.sum(onehot * row_start[None, :], axis=1) + rank
    n_sorted = n_assign + N_EXPERTS * SUBLANES
    asg = jnp.zeros((n_sorted,), jnp.int32).at[pos].set(jnp.arange(n_assign, dtype=jnp.int32), unique_indices=True)
    tid = jnp.arange(n_tiles, dtype=jnp.int32)
    te = jnp.sum((tid[:, None] >= tile_end[None, :]).astype(jnp.int32), axis=1)
    te = jnp.minimum(te, N_EXPERTS - 1)
    active = tid < total
    first = (tid - tile_start[te]) * tm
    tn = jnp.where(active, jnp.clip(counts[te] - first, 0, tm), 0).astype(jnp.int32)
    tb = jnp.where(active, row_start[te] + first, 0).astype(jnp.int32)
    last_e = te[jnp.maximum(total - 1, 0)]
    te = jnp.where(active, te, last_e).astype(jnp.int32)
    return te, tn, tb, asg, pos


def _final_kernel(pos_ref, h_ref, info_ref, g_ref, y_hbm, o_ref, ybuf, sem):
    i = pl.program_id(0)
    tm, half = ybuf.shape[2:]
    slot = lax.rem(i, 2)

    def row_copy(buf, k, r, src):
        return pltpu.make_async_copy(y_hbm.at[pl.ds(src, 1), :], ybuf.at[buf, k, pl.ds(r, 1), :], sem.at[buf])

    def start_gather(step, buf):
        def body(g, carry):
            for u in range(DMA_GROUP):
                r = g * DMA_GROUP + u
                for k in range(2):
                    row_copy(buf, k, r, pos_ref[2 * (step * tm + r) + k]).start()
            return carry
        lax.fori_loop(0, tm // DMA_GROUP, body, 0)

    def wait_gather(buf):
        def body(g, carry):
            for u in range(DMA_GROUP):
                for k in range(2):
                    row_copy(buf, k, g * DMA_GROUP + u, 0).wait()
            return carry
        lax.fori_loop(0, tm // DMA_GROUP, body, 0)

    @pl.when(i == 0)
    def _():
        start_gather(0, 0)

    @pl.when(i + 1 < pl.num_programs(0))
    def _():
        start_gather(i + 1, 1 - slot)

    wait_gather(slot)
    w = info_ref[...]
    w0 = w[:, 0:1]
    w1 = w[:, 1:2]
    y0_lo, y0_hi = _unpack_bf16_pair(ybuf[slot, 0])
    y1_lo, y1_hi = _unpack_bf16_pair(ybuf[slot, 1])
    h_lo = h_ref[:, :half] + w0 * y0_lo + w1 * y1_lo
    h_hi = h_ref[:, half:] + w0 * y0_hi + w1 * y1_hi
    ms = (jnp.sum(h_lo * h_lo, axis=-1, keepdims=True) + jnp.sum(h_hi * h_hi, axis=-1, keepdims=True)) / (2 * half)
    r = lax.rsqrt(ms + RMS_EPS)
    o_ref[:, :half] = h_lo * r * g_ref[:, :half]
    o_ref[:, half:] = h_hi * r * g_ref[:, half:]


def _combine_final(h, y, pos, info, g, tm):
    m, d = h.shape
    assert tm % DMA_GROUP == 0
    grid_spec = pltpu.PrefetchScalarGridSpec(
        num_scalar_prefetch=1,
        grid=(m // tm,),
        in_specs=[pl.BlockSpec((tm, d), lambda i, p: (i, 0)),
                  pl.BlockSpec((tm, LANES), lambda i, p: (i, 0)),
                  pl.BlockSpec((1, d), lambda i, p: (0, 0)),
                  pl.BlockSpec(memory_space=pl.ANY)],
        out_specs=pl.BlockSpec((tm, d), lambda i, p: (i, 0)),
        scratch_shapes=[pltpu.VMEM((2, 2, tm, d // 2), U32), pltpu.SemaphoreType.DMA((2,))],
    )
    return pl.pallas_call(
        _final_kernel,
        grid_spec=grid_spec,
        out_shape=jax.ShapeDtypeStruct((m, d), F32),
        compiler_params=_params("arbitrary"),
        name="combine_final",
    )(pos, h, info, g.reshape(1, d).astype(F32), y)


def kernel(x, meta_tokens, attn_norm_g, w_in, forget_bias, conv_w, conv_b, conv_ln_g, conv_ln_b, w_out,
           ffn_norm_g, w_router_group, b_router_group, w_router_expert, b_router_expert,
           w_expert_gate, w_expert_up, w_expert_down, final_norm_g):
    b, s, d = x.shape
    depth = w_in.shape[0]
    assert depth == 1, "one layer: the meta rows are not carried past the mixer"
    n_heads = forget_bias.shape[1]
    aw = n_heads * HEAD_DIM
    cw = conv_w.shape[2]
    t_rows = b * s
    x2 = x.reshape(t_rows, d)

    xn = _rmsnorm(x2, attn_norm_g[0], BF16, tm=512)
    mn = _rmsnorm(meta_tokens.astype(x.dtype), attn_norm_g[0], BF16, tm=N_META)

    w_t = jnp.transpose(w_in[0])
    qkv, qkv_m = _projection(xn, mn, w_t, 0, 3 * aw, BF16, 1024, 512, "in_proj_qkv")
    fx, f_m = _projection(xn, mn, w_t, 3 * aw, LANES, F32, 1024, LANES, "in_proj_gate")
    u, u_m = _projection(xn, mn, w_t, 3 * aw + n_heads, 2 * cw, BF16, 1024, 512, "in_proj_glu")

    tq = 512
    bias_f = jnp.zeros((1, LANES), F32).at[0, :n_heads].set(forget_bias[0].astype(F32))
    cx, cm = _forget_cumsum(fx.reshape(b, s, LANES), f_m, bias_f)
    cx = jnp.transpose(cx[:, :, :n_heads], (0, 2, 1)).reshape(b, n_heads, s // tq, tq)
    cm = jnp.zeros((n_heads, 1, LANES), F32).at[:, 0, :N_META].set(jnp.transpose(cm[0, :, :n_heads]))
    kvm = jnp.zeros((LANES, 3 * aw), BF16).at[:N_META].set(qkv_m)
    attn = _attention(qkv.reshape(b, s, 3 * aw), kvm, cx, cm, n_heads, tq)

    conv = _conv_branch(u.reshape(b, s, 2 * cw), u_m, conv_w[0], conv_b[0], conv_ln_g[0], conv_ln_b[0], ts=256)

    h1 = _out_projection(attn.reshape(t_rows, aw), conv.reshape(t_rows, cw), w_out, x2, 1024, 512)

    n_r = N_GROUPS + N_EXPERTS
    wr_t = jnp.zeros((LANES, d), F32).at[:N_GROUPS].set(jnp.transpose(w_router_group[0]))
    wr_t = wr_t.at[N_GROUPS:n_r].set(jnp.transpose(w_router_expert[0]))
    br_t = jnp.zeros((LANES, 1), F32).at[:N_GROUPS, 0].set(b_router_group[0]).at[N_GROUPS:n_r, 0].set(b_router_expert[0])
    xp, info_t = _norm_router(h1, ffn_norm_g[0], wr_t, br_t, tm=256)
    gate_w = jnp.zeros((t_rows, LANES), F32).at[:, :2].set(jnp.transpose(info_t[0:2]))
    e_idx = jnp.transpose(info_t[2:4]).astype(jnp.int32)

    tm_moe = 512
    n_tiles = (2 * t_rows) // tm_moe + N_EXPERTS
    te, tn, tb, asg, pos = _routing_tables(e_idx, tm_moe, n_tiles)
    y = _moe(xp, te, tn, tb, asg, w_expert_gate, w_expert_up, w_expert_down, tm_moe)

    out = _combine_final(h1, y, pos, gate_w, final_norm_g, tm=256)
    return out.reshape(b, s, d)
```

```python
import functools

import jax
import jax.numpy as jnp
from jax import lax
from jax.experimental import pallas as pl
from jax.experimental.pallas import tpu as pltpu

F32 = jnp.float32
BF16 = jnp.bfloat16
U32 = jnp.uint32

N_META = 16
HEAD_DIM = 128
CONV_KERNEL = 31
N_GROUPS = 8
EXPERTS_PER_GROUP = 8
N_EXPERTS = N_GROUPS * EXPERTS_PER_GROUP
RMS_EPS = 1e-6
LN_EPS = 1e-5

LANES = 128
SUBLANES = 8
VMEM_LIMIT = 56 * 1024 * 1024
CONV_HALO = 32
CONV_ROWS = 64
MOE_ROW_GRANULE = 64
DMA_GROUP = 8


def _params(*sem):
    return pltpu.CompilerParams(dimension_semantics=sem, vmem_limit_bytes=VMEM_LIMIT)


def _pack_bf16_pair(lo, hi):
    lo_bits = lax.bitcast_convert_type(lo.astype(BF16).astype(F32), U32)
    hi_bits = lax.bitcast_convert_type(hi.astype(BF16).astype(F32), U32)
    return (lo_bits >> 16) | hi_bits


def _unpack_bf16_pair(p):
    lo = lax.bitcast_convert_type(p << 16, F32)
    hi = lax.bitcast_convert_type(p & jnp.uint32(0xFFFF0000), F32)
    return lo, hi


def _rmsnorm_kernel(x_ref, g_ref, o_ref):
    x = x_ref[...].astype(F32)
    y = x * lax.rsqrt(jnp.mean(x * x, axis=-1, keepdims=True) + RMS_EPS)
    o_ref[...] = (y * g_ref[...]).astype(o_ref.dtype)


def _rmsnorm(x, g, out_dtype, tm):
    m, d = x.shape
    return pl.pallas_call(
        _rmsnorm_kernel,
        grid=(m // tm,),
        in_specs=[pl.BlockSpec((tm, d), lambda i: (i, 0)), pl.BlockSpec((1, d), lambda i: (0, 0))],
        out_specs=pl.BlockSpec((tm, d), lambda i: (i, 0)),
        out_shape=jax.ShapeDtypeStruct((m, d), out_dtype),
        compiler_params=_params("parallel"),
        name="rmsnorm",
    )(x, g.reshape(1, d).astype(F32))


def _proj_kernel(a_ref, am_ref, wt_ref, o_ref, om_ref, wb_ref):
    i = pl.program_id(1)
    tn, k = wt_ref.shape
    kc = min(k, 512)

    @pl.when(i == 0)
    def _():
        for c in range(k // kc):
            wb_ref[c * kc:(c + 1) * kc, :] = wt_ref[:, c * kc:(c + 1) * kc].T.astype(BF16)
        om_ref[...] = jnp.dot(am_ref[...], wb_ref[...], preferred_element_type=F32).astype(om_ref.dtype)

    o_ref[...] = jnp.dot(a_ref[...], wb_ref[...], preferred_element_type=F32).astype(o_ref.dtype)


def _projection(a, am, wt, row0, n_cols, out_dtype, tm, tn, name):
    m, k = a.shape
    mm = am.shape[0]
    assert n_cols % tn == 0 and m % tm == 0 and row0 % SUBLANES == 0
    if row0 % tn == 0:
        w_spec = pl.BlockSpec((tn, k), lambda j, i: (row0 // tn + j, 0))
    else:
        w_spec = pl.BlockSpec((pl.Element(tn), pl.Element(k)), lambda j, i: ((row0 // SUBLANES + j * (tn // SUBLANES)) * SUBLANES, 0))
    return pl.pallas_call(
        _proj_kernel,
        grid=(n_cols // tn, m // tm),
        in_specs=[pl.BlockSpec((tm, k), lambda j, i: (i, 0)),
                  pl.BlockSpec((mm, k), lambda j, i: (0, 0)),
                  w_spec],
        out_specs=[pl.BlockSpec((tm, tn), lambda j, i: (i, j)), pl.BlockSpec((mm, tn), lambda j, i: (0, j))],
        out_shape=[jax.ShapeDtypeStruct((m, n_cols), out_dtype), jax.ShapeDtypeStruct((mm, n_cols), out_dtype)],
        scratch_shapes=[pltpu.VMEM((k, tn), BF16)],
        compiler_params=_params("parallel", "arbitrary"),
        name=name,
    )(a, am, wt)


def _log_sigmoid(z):
    return jnp.minimum(z, 0.0) - jnp.log1p(jnp.exp(-jnp.abs(z)))


def _split3_bf16(x):
    hi = x.astype(BF16)
    r1 = x - hi.astype(F32)
    mid = r1.astype(BF16)
    lo = (r1 - mid.astype(F32)).astype(BF16)
    return hi, mid, lo


def _cumsum_rows(l, tril):
    dot = functools.partial(jnp.dot, preferred_element_type=F32)
    hi, mid, lo = _split3_bf16(l)
    return dot(tril, hi) + dot(tril, mid) + dot(tril, lo)


def _tril(n):
    r = lax.broadcasted_iota(jnp.int32, (n, n), 0)
    c = lax.broadcasted_iota(jnp.int32, (n, n), 1)
    return jnp.where(r >= c, 1.0, 0.0).astype(BF16)


def _cumsum_kernel(fx_ref, fm_ref, b_ref, cx_ref, cm_ref, *, blk):
    bias = b_ref[...]
    cm = _cumsum_rows(_log_sigmoid(fm_ref[...] + bias), _tril(N_META))
    cm_ref[0] = cm
    carry = cm[N_META - 1:N_META, :]
    tril = _tril(blk)
    for j in range(fx_ref.shape[1] // blk):
        l = _log_sigmoid(fx_ref[0, j * blk:(j + 1) * blk, :] + bias)
        c = _cumsum_rows(l, tril) + carry
        cx_ref[0, j * blk:(j + 1) * blk, :] = c
        carry = c[blk - 1:blk, :]


def _forget_cumsum(fx, fm, bias):
    b, s, _ = fx.shape
    return pl.pallas_call(
        functools.partial(_cumsum_kernel, blk=256),
        grid=(b,),
        in_specs=[pl.BlockSpec((1, s, LANES), lambda i: (i, 0, 0)),
                  pl.BlockSpec((N_META, LANES), lambda i: (0, 0)),
                  pl.BlockSpec((1, LANES), lambda i: (0, 0))],
        out_specs=[pl.BlockSpec((1, s, LANES), lambda i: (i, 0, 0)),
                   pl.BlockSpec((1, N_META, LANES), lambda i: (i, 0, 0))],
        out_shape=[jax.ShapeDtypeStruct((b, s, LANES), F32), jax.ShapeDtypeStruct((b, N_META, LANES), F32)],
        compiler_params=_params("parallel"),
        name="forget_cumsum",
    )(fx, fm, bias)


def _attn_kernel(q_ref, k_ref, v_ref, km_ref, vm_ref, cx_ref, cm_ref, o_ref, *, tq):
    i = pl.program_id(2)
    q = q_ref[0]
    scale = HEAD_DIM ** -0.5

    def scores(kb, cb):
        s = lax.dot_general(q, kb, (((1,), (1,)), ((), ())), preferred_element_type=F32)
        return s * scale - cb

    s = scores(km_ref[...], cm_ref[0])
    col = lax.broadcasted_iota(jnp.int32, s.shape, 1)
    s = jnp.where(col < N_META, s, -jnp.inf)
    m = jnp.max(s, axis=-1, keepdims=True)
    p = jnp.exp(s - m)
    l = jnp.sum(p, axis=-1, keepdims=True)
    acc = jnp.dot(p.astype(BF16), vm_ref[...], preferred_element_type=F32)

    def update(s, vb, m, l, acc):
        m_new = jnp.maximum(m, jnp.max(s, axis=-1, keepdims=True))
        alpha = jnp.exp(m - m_new)
        p = jnp.exp(s - m_new)
        l = alpha * l + jnp.sum(p, axis=-1, keepdims=True)
        acc = alpha * acc + jnp.dot(p.astype(BF16), vb, preferred_element_type=F32)
        return m_new, l, acc

    def block(j):
        off = pl.multiple_of(j * tq, tq)
        return (scores(k_ref[0, pl.ds(off, tq), :], cx_ref[0, 0, pl.ds(j, 1), :]), v_ref[0, pl.ds(off, tq), :])

    def body(j, carry):
        s, vb = block(j)
        return update(s, vb, *carry)

    m, l, acc = lax.fori_loop(0, i, body, (m, l, acc))
    s, vb = block(i)
    row = lax.broadcasted_iota(jnp.int32, s.shape, 0)
    col = lax.broadcasted_iota(jnp.int32, s.shape, 1)
    s = jnp.where(row >= col, s, -jnp.inf)
    m, l, acc = update(s, vb, m, l, acc)
    o_ref[0] = (acc / l).astype(o_ref.dtype)


def _attention(qkv, kvm, cx, cm, n_heads, tq):
    b, s, _ = qkv.shape
    nk = s // tq
    blk = lambda f: pl.BlockSpec((1, tq, HEAD_DIM), f)
    return pl.pallas_call(
        functools.partial(_attn_kernel, tq=tq),
        grid=(b, n_heads, nk),
        in_specs=[blk(lambda bi, h, i: (bi, i, h)),
                  pl.BlockSpec((1, s, HEAD_DIM), lambda bi, h, i: (bi, 0, n_heads + h)),
                  pl.BlockSpec((1, s, HEAD_DIM), lambda bi, h, i: (bi, 0, 2 * n_heads + h)),
                  pl.BlockSpec((LANES, HEAD_DIM), lambda bi, h, i: (0, n_heads + h)),
                  pl.BlockSpec((LANES, HEAD_DIM), lambda bi, h, i: (0, 2 * n_heads + h)),
                  pl.BlockSpec((1, 1, nk, tq), lambda bi, h, i: (bi, h, 0, 0)),
                  pl.BlockSpec((1, 1, LANES), lambda bi, h, i: (h, 0, 0))],
        out_specs=blk(lambda bi, h, i: (bi, i, h)),
        out_shape=jax.ShapeDtypeStruct((b, s, n_heads * HEAD_DIM), BF16),
        compiler_params=_params("parallel", "parallel", "parallel"),
        name="fox_attention",
    )(qkv, qkv, qkv, kvm, kvm, cx, cm)


def _conv_kernel(ua_ref, ug_ref, ma_ref, mg_ref, w_ref, cb_ref, g_ref, b_ref, o_ref, hb, hs, cbuf, *, ts):
    t = pl.program_id(1)
    nch = hb.shape[0]
    off = CONV_HALO - (CONV_KERNEL - 1)
    span = ts + CONV_HALO - SUBLANES
    lanes = lambda j: slice(j * LANES, (j + 1) * LANES)

    @pl.when(t == 0)
    def _():
        for j in range(nch):
            hb[j, 0:CONV_HALO - N_META, :] = jnp.zeros((CONV_HALO - N_META, LANES), F32)
            hb[j, CONV_HALO - N_META:CONV_HALO, :] = (
                ma_ref[:, lanes(j)].astype(F32) * jax.nn.sigmoid(mg_ref[:, lanes(j)].astype(F32)))

    for j in range(nch):
        hb[j, CONV_HALO:CONV_HALO + ts, :] = (
            ua_ref[0, :, lanes(j)].astype(F32) * jax.nn.sigmoid(ug_ref[0, :, lanes(j)].astype(F32)))

    def lane_chunk(j, carry):
        for s in range(1, SUBLANES):
            hs[s - 1] = hb[j, pl.ds(s, span), :]

        def row_chunk(i, c2):
            r0 = pl.multiple_of(i * CONV_ROWS, CONV_ROWS)
            tile_rows = lambda v: jnp.concatenate([v] * (CONV_ROWS // SUBLANES), axis=0)
            acc = tile_rows(cb_ref[j])
            for k in range(CONV_KERNEL):
                a, s = divmod(off + k, SUBLANES)
                rows = pl.ds(r0 + SUBLANES * a, CONV_ROWS)
                src = hb[j, rows, :] if s == 0 else hs[s - 1, rows, :]
                acc = acc + tile_rows(w_ref[j, k]) * src
            cbuf[j, pl.ds(r0, CONV_ROWS), :] = acc
            return c2

        lax.fori_loop(0, ts // CONV_ROWS, row_chunk, 0)
        return carry

    lax.fori_loop(0, nch, lane_chunk, 0)

    for j in range(nch):
        hb[j, 0:CONV_HALO, :] = hb[j, ts:ts + CONV_HALO, :]

    c = cbuf[...]
    n_ch = nch * LANES
    mu = jnp.sum(jnp.sum(c, axis=0), axis=-1, keepdims=True) / n_ch
    xc = c - mu[None]
    var = jnp.sum(jnp.sum(xc * xc, axis=0), axis=-1, keepdims=True) / n_ch
    y = xc * lax.rsqrt(var + LN_EPS)[None] * g_ref[...] + b_ref[...]
    y = y * jax.nn.sigmoid(y)
    for j in range(nch):
        o_ref[0, :, lanes(j)] = y[j].astype(o_ref.dtype)


def _conv_branch(u, um, conv_w, conv_b, ln_g, ln_b, ts):
    b, s, c2 = u.shape
    c = c2 // 2
    nch = c // LANES
    chunked = lambda v: jnp.transpose(v.astype(F32).reshape(-1, nch, LANES), (1, 0, 2))
    w_rep = jnp.broadcast_to(chunked(conv_w)[:, :, None, :], (nch, CONV_KERNEL, SUBLANES, LANES))
    cb_rep = jnp.broadcast_to(chunked(conv_b), (nch, SUBLANES, LANES))
    full = lambda shape: pl.BlockSpec(shape, lambda bi, t: (0,) * len(shape))
    return pl.pallas_call(
        functools.partial(_conv_kernel, ts=ts),
        grid=(b, s // ts),
        in_specs=[pl.BlockSpec((1, ts, c), lambda bi, t: (bi, t, 0)),
                  pl.BlockSpec((1, ts, c), lambda bi, t: (bi, t, 1)),
                  pl.BlockSpec((N_META, c), lambda bi, t: (0, 0)),
                  pl.BlockSpec((N_META, c), lambda bi, t: (0, 1)),
                  full((nch, CONV_KERNEL, SUBLANES, LANES)),
                  full((nch, SUBLANES, LANES)),
                  full((nch, 1, LANES)),
                  full((nch, 1, LANES))],
        out_specs=pl.BlockSpec((1, ts, c), lambda bi, t: (bi, t, 0)),
        out_shape=jax.ShapeDtypeStruct((b, s, c), BF16),
        scratch_shapes=[pltpu.VMEM((nch, ts + CONV_HALO, LANES), F32),
                        pltpu.VMEM((SUBLANES - 1, ts + CONV_HALO - SUBLANES, LANES), F32),
                        pltpu.VMEM((nch, ts, LANES), F32)],
        compiler_params=_params("parallel", "arbitrary"),
        name="conformer_conv",
    )(u, u, um, um, w_rep, cb_rep, chunked(ln_g), chunked(ln_b))


def _outproj_kernel(a_ref, c_ref, wa_ref, wc_ref, x_ref, o_ref, wab_ref, wcb_ref):
    @pl.when(pl.program_id(1) == 0)
    def _():
        wab_ref[...] = wa_ref[0].astype(BF16)
        wcb_ref[...] = wc_ref[0].astype(BF16)

    acc = jnp.dot(a_ref[...], wab_ref[...], preferred_element_type=F32)
    acc = acc + jnp.dot(c_ref[...], wcb_ref[...], preferred_element_type=F32)
    o_ref[...] = x_ref[...] + acc


def _out_projection(a, c, w, x, tm, tn):
    m, ka = a.shape
    kc = c.shape[1]
    n = w.shape[2]
    assert ka == kc
    return pl.pallas_call(
        _outproj_kernel,
        grid=(n // tn, m // tm),
        in_specs=[pl.BlockSpec((tm, ka), lambda j, i: (i, 0)),
                  pl.BlockSpec((tm, kc), lambda j, i: (i, 0)),
                  pl.BlockSpec((1, ka, tn), lambda j, i: (0, 0, j)),
                  pl.BlockSpec((1, kc, tn), lambda j, i: (0, 1, j)),
                  pl.BlockSpec((tm, tn), lambda j, i: (i, j))],
        out_specs=pl.BlockSpec((tm, tn), lambda j, i: (i, j)),
        out_shape=jax.ShapeDtypeStruct((m, n), F32),
        scratch_shapes=[pltpu.VMEM((ka, tn), BF16), pltpu.VMEM((kc, tn), BF16)],
        compiler_params=_params("parallel", "arbitrary"),
        name="out_projection",
    )(a, c, w, w, x)


def _router_kernel(h_ref, g_ref, wr_ref, br_ref, xp_ref, info_ref):
    h = h_ref[...]
    xn = h * lax.rsqrt(jnp.mean(h * h, axis=-1, keepdims=True) + RMS_EPS) * g_ref[...]
    half = xn.shape[1] // 2
    xp_ref[...] = _pack_bf16_pair(xn[:, :half], xn[:, half:])
    logits = jnp.dot(xn, wr_ref[...], precision=lax.Precision.HIGHEST, preferred_element_type=F32) + br_ref[...]
    lane = lax.broadcasted_iota(jnp.int32, logits.shape, 1).astype(F32)
    rmax = lambda v: jnp.max(v, axis=-1, keepdims=True)
    rmin = lambda v: jnp.min(v, axis=-1, keepdims=True)
    rsum = lambda v: jnp.sum(v, axis=-1, keepdims=True)
    far = float(LANES)

    is_g = lane < N_GROUPS
    lg = jnp.where(is_g, logits, -jnp.inf)
    mg = rmax(lg)
    g_idx = rmin(jnp.where(lg == mg, lane, far))
    g_w = 1.0 / rsum(jnp.where(is_g, jnp.exp(lg - mg), 0.0))

    lo = N_GROUPS + EXPERTS_PER_GROUP * g_idx
    in_e = jnp.logical_and(lane >= lo, lane < lo + EXPERTS_PER_GROUP)
    le = jnp.where(in_e, logits, -jnp.inf)
    pe = jnp.where(in_e, jnp.exp(le - rmax(le)), 0.0)
    prob = jnp.where(in_e, pe / rsum(pe), -1.0)
    p1 = rmax(prob)
    i1 = rmin(jnp.where(prob == p1, lane, far))
    prob2 = jnp.where(lane == i1, -1.0, prob)
    p2 = rmax(prob2)
    i2 = rmin(jnp.where(prob2 == p2, lane, far))
    den = p1 + p2
    w1 = g_w * (p1 / den)
    w2 = g_w * (p2 / den)
    info = jnp.where(lane == 0, w1, jnp.where(lane == 1, w2, jnp.where(
        lane == 2, i1 - N_GROUPS, jnp.where(lane == 3, i2 - N_GROUPS, 0.0))))
    info_ref[...] = info


def _norm_router(h, g, wr, br, tm):
    m, d = h.shape
    return pl.pallas_call(
        _router_kernel,
        grid=(m // tm,),
        in_specs=[pl.BlockSpec((tm, d), lambda i: (i, 0)),
                  pl.BlockSpec((1, d), lambda i: (0, 0)),
                  pl.BlockSpec((d, LANES), lambda i: (0, 0)),
                  pl.BlockSpec((1, LANES), lambda i: (0, 0))],
        out_specs=[pl.BlockSpec((tm, d // 2), lambda i: (i, 0)), pl.BlockSpec((tm, LANES), lambda i: (i, 0))],
        out_shape=[jax.ShapeDtypeStruct((m, d // 2), U32), jax.ShapeDtypeStruct((m, LANES), F32)],
        compiler_params=_params("parallel"),
        name="norm_router",
    )(h, g.reshape(1, d).astype(F32), wr, br)


def _moe_kernel(te_ref, tn_ref, tb_ref, pos_ref, xp_hbm, wg_ref, wu_ref, wd_ref, y_hbm,
                xq, xb, gacc, uacc, wd0, ystage, asg_ref, pend, gsem, ssem, *, y_tail_rows):
    t = pl.program_id(0)
    f = pl.program_id(1)
    n_tiles = pl.num_programs(0)
    n = tn_ref[t]
    tm, half = ystage.shape
    groups = lambda cnt: (cnt + DMA_GROUP - 1) // DMA_GROUP

    def gather_copy(r, tok):
        return pltpu.make_async_copy(xp_hbm.at[pl.ds(tok, 1), :], xq.at[pl.ds(r, 1), :], gsem)

    def start_gather(tile):
        base = tb_ref[tile]
        cnt = tn_ref[tile]

        def body(g, carry):
            for u in range(DMA_GROUP):
                r = g * DMA_GROUP + u
                gather_copy(r, asg_ref[base + jnp.minimum(r, cnt - 1)] >> 1).start()
            return carry
        lax.fori_loop(0, groups(cnt), body, 0)

    def wait_gather(cnt):
        def body(g, carry):
            for u in range(DMA_GROUP):
                gather_copy(g * DMA_GROUP + u, 0).wait()
            return carry
        lax.fori_loop(0, groups(cnt), body, 0)

    n_granules = (n + MOE_ROW_GRANULE - 1) // MOE_ROW_GRANULE
    row_counts = [k * MOE_ROW_GRANULE for k in range(1, tm // MOE_ROW_GRANULE + 1)]

    def out_copy(m, base):
        return pltpu.make_async_copy(ystage.at[pl.ds(0, m), :], y_hbm.at[pl.ds(base, m), :], ssem)

    def start_scatter(m):
        out_copy(m, pl.multiple_of(tb_ref[t], SUBLANES)).start()
        pend[0] = m

    def wait_scatter():
        for m in row_counts:
            @pl.when(pend[0] == m)
            def _(m=m):
                out_copy(m, 0).wait()
        pend[0] = 0

    @pl.when(jnp.logical_and(t == 0, f == 0))
    def _():
        xq[...] = jnp.zeros(xq.shape, xq.dtype)
        pend[0] = 0
        for part in range(y_tail_rows // tm):
            fill = pltpu.make_async_copy(xq, y_hbm.at[pl.ds(y_hbm.shape[0] - (part + 1) * tm, tm), :], ssem)
            fill.start()
            fill.wait()

        def invert(g, carry):
            for u in range(DMA_GROUP):
                a = g * DMA_GROUP + u
                asg_ref[pos_ref[a]] = a
            return carry
        lax.fori_loop(0, pos_ref.shape[0] // DMA_GROUP, invert, 0)
        start_gather(0)

    def unpack_rows(m):
        lo, hi = _unpack_bf16_pair(xq[0:m, :])
        xb[0, 0:m, :] = lo.astype(BF16)
        xb[1, 0:m, :] = hi.astype(BF16)

    @pl.when(jnp.logical_and(f == 0, n > 0))
    def _():
        wait_gather(n)
        for m in row_counts:
            pl.when(n_granules * MOE_ROW_GRANULE == m)(functools.partial(unpack_rows, m))
        wd0[...] = wd_ref[0].astype(BF16)

    @pl.when(jnp.logical_and(f == 0, t + 1 < n_tiles))
    def _():
        start_gather(t + 1)

    def ffn_rows(m):
        x = xb[f, 0:m, :]
        g_part = jnp.dot(x, wg_ref[0].astype(BF16), preferred_element_type=F32)
        u_part = jnp.dot(x, wu_ref[0].astype(BF16), preferred_element_type=F32)

        @pl.when(f == 0)
        def _():
            gacc[0:m, :] = g_part
            uacc[0:m, :] = u_part

        @pl.when(f == 1)
        def _():
            g = gacc[0:m, :] + g_part
            u = uacc[0:m, :] + u_part
            hid = ((g * jax.nn.sigmoid(g)) * u).astype(BF16)
            fh = wd0.shape[0]
            out = jnp.dot(hid[:, :fh], wd0[...], preferred_element_type=F32)
            out = out + jnp.dot(hid[:, fh:], wd_ref[0].astype(BF16), preferred_element_type=F32)
            wait_scatter()
            ystage[0:m, :] = _pack_bf16_pair(out[:, :half], out[:, half:])
            start_scatter(m)

    for m in row_counts:
        pl.when(n_granules * MOE_ROW_GRANULE == m)(functools.partial(ffn_rows, m))

    @pl.when(jnp.logical_and(f == 1, t == n_tiles - 1))
    def _():
        wait_scatter()


def _moe(xp, tile_e, tile_n, tile_base, pos, n_sorted, w_gate, w_up, w_down, tm):
    assert tm % MOE_ROW_GRANULE == 0 and MOE_ROW_GRANULE % DMA_GROUP == 0 and pos.shape[0] % DMA_GROUP == 0
    t_rows, half = xp.shape
    d = 2 * half
    n_tiles = tile_e.shape[0]
    ff = w_gate.shape[3]
    fh = ff // 2
    y_rows = n_sorted + tm
    y_tail_rows = -(-(y_rows - 2 * t_rows) // tm) * tm

    def fidx(f, tn, t):
        return jnp.where(tn[t] > 0, f, 1)

    grid_spec = pltpu.PrefetchScalarGridSpec(
        num_scalar_prefetch=4,
        grid=(n_tiles, 2),
        in_specs=[pl.BlockSpec(memory_space=pl.ANY),
                  pl.BlockSpec((1, 1, half, ff), lambda t, f, te, tn, tb, ds: (0, te[t], fidx(f, tn, t), 0)),
                  pl.BlockSpec((1, 1, half, ff), lambda t, f, te, tn, tb, ds: (0, te[t], fidx(f, tn, t), 0)),
                  pl.BlockSpec((1, 1, fh, d), lambda t, f, te, tn, tb, ds: (0, te[t], fidx(f, tn, t), 0))],
        out_specs=pl.BlockSpec(memory_space=pl.ANY),
        scratch_shapes=[pltpu.VMEM((tm, half), U32), pltpu.VMEM((2, tm, half), BF16),
                        pltpu.VMEM((tm, ff), F32), pltpu.VMEM((tm, ff), F32), pltpu.VMEM((fh, d), BF16),
                        pltpu.VMEM((tm, half), U32), pltpu.SMEM((n_sorted,), jnp.int32), pltpu.SMEM((1,), jnp.int32),
                        pltpu.SemaphoreType.DMA(()), pltpu.SemaphoreType.DMA(())],
    )
    return pl.pallas_call(
        functools.partial(_moe_kernel_4d, y_tail_rows=y_tail_rows),
        grid_spec=grid_spec,
        out_shape=jax.ShapeDtypeStruct((y_rows, half), U32),
        compiler_params=_params("arbitrary", "arbitrary"),
        name="sparse_moe",
    )(tile_e, tile_n, tile_base, pos, xp, w_gate, w_up, w_down)


def _moe_kernel_4d(te_ref, tn_ref, tb_ref, pos_ref, xp_hbm, wg_ref, wu_ref, wd_ref, *rest, y_tail_rows):
    _moe_kernel(te_ref, tn_ref, tb_ref, pos_ref, xp_hbm, wg_ref.at[0], wu_ref.at[0], wd_ref.at[0], *rest,
                y_tail_rows=y_tail_rows)


def _routing_tables(e_idx, tm, n_tiles):
    flat_e = e_idx.reshape(-1)
    n_assign = flat_e.shape[0]
    onehot = (flat_e[:, None] == jnp.arange(N_EXPERTS, dtype=jnp.int32)[None, :]).astype(jnp.int32)
    csum = jnp.cumsum(onehot, axis=0)
    rank = jnp.sum(csum * onehot, axis=1) - 1
    counts = csum[-1]
    seg = (counts + SUBLANES - 1) // SUBLANES * SUBLANES
    row_end = jnp.cumsum(seg)
    row_start = row_end - seg
    tiles_per_e = (counts + tm - 1) // tm
    tile_end = jnp.cumsum(tiles_per_e)
    tile_start = tile_end - tiles_per_e
    total = tile_end[-1]
    pos = jnp.sum(onehot * row_start[None, :], axis=1) + rank
    n_sorted = n_assign + N_EXPERTS * SUBLANES
    tid = jnp.arange(n_tiles, dtype=jnp.int32)
    te = jnp.sum((tid[:, None] >= tile_end[None, :]).astype(jnp.int32), axis=1)
    te = jnp.minimum(te, N_EXPERTS - 1)
    active = tid < total
    first = (tid - tile_start[te]) * tm
    tn = jnp.where(active, jnp.clip(counts[te] - first, 0, tm), 0).astype(jnp.int32)
    tb = jnp.where(active, row_start[te] + first, 0).astype(jnp.int32)
    last_e = te[jnp.maximum(total - 1, 0)]
    te = jnp.where(active, te, last_e).astype(jnp.int32)
    return te, tn, tb, pos.astype(jnp.int32), n_sorted


def _final_kernel(pos_ref, h_ref, info_ref, g_ref, y_hbm, o_ref, ybuf, sem):
    i = pl.program_id(0)
    tm, half = ybuf.shape[2:]
    slot = lax.rem(i, 2)

    def row_copy(buf, k, r, src):
        return pltpu.make_async_copy(y_hbm.at[pl.ds(src, 1), :], ybuf.at[buf, k, pl.ds(r, 1), :], sem.at[buf])

    def start_gather(step, buf):
        def body(g, carry):
            for u in range(DMA_GROUP):
                r = g * DMA_GROUP + u
                for k in range(2):
                    row_copy(buf, k, r, pos_ref[2 * (step * tm + r) + k]).start()
            return carry
        lax.fori_loop(0, tm // DMA_GROUP, body, 0)

    def wait_gather(buf):
        def body(g, carry):
            for u in range(DMA_GROUP):
                for k in range(2):
                    row_copy(buf, k, g * DMA_GROUP + u, 0).wait()
            return carry
        lax.fori_loop(0, tm // DMA_GROUP, body, 0)

    @pl.when(i == 0)
    def _():
        start_gather(0, 0)

    @pl.when(i + 1 < pl.num_programs(0))
    def _():
        start_gather(i + 1, 1 - slot)

    wait_gather(slot)
    w = info_ref[...]
    w0 = w[:, 0:1]
    w1 = w[:, 1:2]
    y0_lo, y0_hi = _unpack_bf16_pair(ybuf[slot, 0])
    y1_lo, y1_hi = _unpack_bf16_pair(ybuf[slot, 1])
    h_lo = h_ref[:, :half] + w0 * y0_lo + w1 * y1_lo
    h_hi = h_ref[:, half:] + w0 * y0_hi + w1 * y1_hi
    ms = (jnp.sum(h_lo * h_lo, axis=-1, keepdims=True) + jnp.sum(h_hi * h_hi, axis=-1, keepdims=True)) / (2 * half)
    r = lax.rsqrt(ms + RMS_EPS)
    o_ref[:, :half] = h_lo * r * g_ref[:, :half]
    o_ref[:, half:] = h_hi * r * g_ref[:, half:]


def _combine_final(h, y, pos, info, g, tm):
    m, d = h.shape
    assert tm % DMA_GROUP == 0
    grid_spec = pltpu.PrefetchScalarGridSpec(
        num_scalar_prefetch=1,
        grid=(m // tm,),
        in_specs=[pl.BlockSpec((tm, d), lambda i, p: (i, 0)),
                  pl.BlockSpec((tm, LANES), lambda i, p: (i, 0)),
                  pl.BlockSpec((1, d), lambda i, p: (0, 0)),
                  pl.BlockSpec(memory_space=pl.ANY)],
        out_specs=pl.BlockSpec((tm, d), lambda i, p: (i, 0)),
        scratch_shapes=[pltpu.VMEM((2, 2, tm, d // 2), U32), pltpu.SemaphoreType.DMA((2,))],
    )
    return pl.pallas_call(
        _final_kernel,
        grid_spec=grid_spec,
        out_shape=jax.ShapeDtypeStruct((m, d), F32),
        compiler_params=_params("arbitrary"),
        name="combine_final",
    )(pos, h, info, g.reshape(1, d).astype(F32), y)


def kernel(x, meta_tokens, attn_norm_g, w_in, forget_bias, conv_w, conv_b, conv_ln_g, conv_ln_b, w_out,
           ffn_norm_g, w_router_group, b_router_group, w_router_expert, b_router_expert,
           w_expert_gate, w_expert_up, w_expert_down, final_norm_g):
    b, s, d = x.shape
    depth = w_in.shape[0]
    assert depth == 1, "one layer: the meta rows are not carried past the mixer"
    n_heads = forget_bias.shape[1]
    aw = n_heads * HEAD_DIM
    cw = conv_w.shape[2]
    t_rows = b * s
    x2 = x.reshape(t_rows, d)

    xn = _rmsnorm(x2, attn_norm_g[0], BF16, tm=512)
    mn = _rmsnorm(meta_tokens.astype(x.dtype), attn_norm_g[0], BF16, tm=N_META)

    w_t = jnp.transpose(w_in[0])
    qkv, qkv_m = _projection(xn, mn, w_t, 0, 3 * aw, BF16, 1024, 512, "in_proj_qkv")
    fx, f_m = _projection(xn, mn, w_t, 3 * aw, LANES, F32, 1024, LANES, "in_proj_gate")
    u, u_m = _projection(xn, mn, w_t, 3 * aw + n_heads, 2 * cw, BF16, 1024, 512, "in_proj_glu")

    tq = 512
    bias_f = jnp.zeros((1, LANES), F32).at[0, :n_heads].set(forget_bias[0].astype(F32))
    cx, cm = _forget_cumsum(fx.reshape(b, s, LANES), f_m, bias_f)
    cx = jnp.transpose(cx[:, :, :n_heads], (0, 2, 1)).reshape(b, n_heads, s // tq, tq)
    cm = jnp.zeros((n_heads, 1, LANES), F32).at[:, 0, :N_META].set(jnp.transpose(cm[0, :, :n_heads]))
    kvm = jnp.zeros((LANES, 3 * aw), BF16).at[:N_META].set(qkv_m)
    attn = _attention(qkv.reshape(b, s, 3 * aw), kvm, cx, cm, n_heads, tq)

    conv = _conv_branch(u.reshape(b, s, 2 * cw), u_m, conv_w[0], conv_b[0], conv_ln_g[0], conv_ln_b[0], ts=256)

    h1 = _out_projection(attn.reshape(t_rows, aw), conv.reshape(t_rows, cw), w_out, x2, 1024, 512)

    n_r = N_GROUPS + N_EXPERTS
    wr = jnp.zeros((d, LANES), F32).at[:, :N_GROUPS].set(w_router_group[0]).at[:, N_GROUPS:n_r].set(w_router_expert[0])
    br = jnp.zeros((1, LANES), F32).at[0, :N_GROUPS].set(b_router_group[0]).at[0, N_GROUPS:n_r].set(b_router_expert[0])
    xp, info = _norm_router(h1, ffn_norm_g[0], wr, br, tm=256)

    tm_moe = 512
    n_tiles = (2 * t_rows) // tm_moe + N_EXPERTS
    te, tn, tb, pos, n_sorted = _routing_tables(info[:, 2:4].astype(jnp.int32), tm_moe, n_tiles)
    y = _moe(xp, te, tn, tb, pos, n_sorted, w_expert_gate, w_expert_up, w_expert_down, tm_moe)

    out = _combine_final(h1, y, pos, info, final_norm_g, tm=256)
    return out.reshape(b, s, d)
```

```python
import functools

import jax
import jax.numpy as jnp
from jax import lax
from jax.experimental import pallas as pl
from jax.experimental.pallas import tpu as pltpu

F32 = jnp.float32
BF16 = jnp.bfloat16
U32 = jnp.uint32

N_META = 16
HEAD_DIM = 128
CONV_KERNEL = 31
N_GROUPS = 8
EXPERTS_PER_GROUP = 8
N_EXPERTS = N_GROUPS * EXPERTS_PER_GROUP
RMS_EPS = 1e-6
LN_EPS = 1e-5

LANES = 128
SUBLANES = 8
VMEM_LIMIT = 56 * 1024 * 1024
CONV_HALO = 32
CONV_ROWS = 256
MOE_ROW_GRANULE = 64
DMA_GROUP = 8


def _params(*sem):
    return pltpu.CompilerParams(dimension_semantics=sem, vmem_limit_bytes=VMEM_LIMIT)


def _pack_bf16_pair(lo, hi):
    lo_bits = lax.bitcast_convert_type(lo.astype(BF16).astype(F32), U32)
    hi_bits = lax.bitcast_convert_type(hi.astype(BF16).astype(F32), U32)
    return (lo_bits >> 16) | hi_bits


def _unpack_bf16_pair(p):
    lo = lax.bitcast_convert_type(p << 16, F32)
    hi = lax.bitcast_convert_type(p & jnp.uint32(0xFFFF0000), F32)
    return lo, hi


def _rmsnorm_kernel(x_ref, g_ref, o_ref):
    x = x_ref[...].astype(F32)
    y = x * lax.rsqrt(jnp.mean(x * x, axis=-1, keepdims=True) + RMS_EPS)
    o_ref[...] = (y * g_ref[...]).astype(o_ref.dtype)


def _rmsnorm(x, g, out_dtype, tm):
    m, d = x.shape
    return pl.pallas_call(
        _rmsnorm_kernel,
        grid=(m // tm,),
        in_specs=[pl.BlockSpec((tm, d), lambda i: (i, 0)), pl.BlockSpec((1, d), lambda i: (0, 0))],
        out_specs=pl.BlockSpec((tm, d), lambda i: (i, 0)),
        out_shape=jax.ShapeDtypeStruct((m, d), out_dtype),
        compiler_params=_params("parallel"),
        name="rmsnorm",
    )(x, g.reshape(1, d).astype(F32))


def _proj_kernel(a_ref, am_ref, wt_ref, o_ref, om_ref, wb_ref):
    i = pl.program_id(1)
    tn, k = wt_ref.shape
    kc = min(k, 512)

    @pl.when(i == 0)
    def _():
        for c in range(k // kc):
            wb_ref[c * kc:(c + 1) * kc, :] = wt_ref[:, c * kc:(c + 1) * kc].T.astype(BF16)
        om_ref[...] = jnp.dot(am_ref[...], wb_ref[...], preferred_element_type=F32).astype(om_ref.dtype)

    o_ref[...] = jnp.dot(a_ref[...], wb_ref[...], preferred_element_type=F32).astype(o_ref.dtype)


def _projection(a, am, wt, row0, n_cols, out_dtype, tm, tn, name):
    m, k = a.shape
    mm = am.shape[0]
    assert n_cols % tn == 0 and m % tm == 0 and row0 % SUBLANES == 0
    if row0 % tn == 0:
        w_spec = pl.BlockSpec((tn, k), lambda j, i: (row0 // tn + j, 0))
    else:
        w_spec = pl.BlockSpec((pl.Element(tn), pl.Element(k)), lambda j, i: ((row0 // SUBLANES + j * (tn // SUBLANES)) * SUBLANES, 0))
    return pl.pallas_call(
        _proj_kernel,
        grid=(n_cols // tn, m // tm),
        in_specs=[pl.BlockSpec((tm, k), lambda j, i: (i, 0)),
                  pl.BlockSpec((mm, k), lambda j, i: (0, 0)),
                  w_spec],
        out_specs=[pl.BlockSpec((tm, tn), lambda j, i: (i, j)), pl.BlockSpec((mm, tn), lambda j, i: (0, j))],
        out_shape=[jax.ShapeDtypeStruct((m, n_cols), out_dtype), jax.ShapeDtypeStruct((mm, n_cols), out_dtype)],
        scratch_shapes=[pltpu.VMEM((k, tn), BF16)],
        compiler_params=_params("parallel", "arbitrary"),
        name=name,
    )(a, am, wt)


def _log_sigmoid(z):
    return jnp.minimum(z, 0.0) - jnp.log1p(jnp.exp(-jnp.abs(z)))


def _split3_bf16(x):
    hi = x.astype(BF16)
    r1 = x - hi.astype(F32)
    mid = r1.astype(BF16)
    lo = (r1 - mid.astype(F32)).astype(BF16)
    return hi, mid, lo


def _cumsum_rows(l, tril):
    dot = functools.partial(jnp.dot, preferred_element_type=F32)
    hi, mid, lo = _split3_bf16(l)
    return dot(tril, hi) + dot(tril, mid) + dot(tril, lo)


def _tril(n):
    r = lax.broadcasted_iota(jnp.int32, (n, n), 0)
    c = lax.broadcasted_iota(jnp.int32, (n, n), 1)
    return jnp.where(r >= c, 1.0, 0.0).astype(BF16)


def _cumsum_kernel(fx_ref, fm_ref, b_ref, cx_ref, cm_ref, *, blk):
    bias = b_ref[...]
    cm = _cumsum_rows(_log_sigmoid(fm_ref[...] + bias), _tril(N_META))
    cm_ref[0] = cm
    carry = cm[N_META - 1:N_META, :]
    tril = _tril(blk)
    for j in range(fx_ref.shape[1] // blk):
        l = _log_sigmoid(fx_ref[0, j * blk:(j + 1) * blk, :] + bias)
        c = _cumsum_rows(l, tril) + carry
        cx_ref[0, j * blk:(j + 1) * blk, :] = c
        carry = c[blk - 1:blk, :]


def _forget_cumsum(fx, fm, bias):
    b, s, _ = fx.shape
    return pl.pallas_call(
        functools.partial(_cumsum_kernel, blk=256),
        grid=(b,),
        in_specs=[pl.BlockSpec((1, s, LANES), lambda i: (i, 0, 0)),
                  pl.BlockSpec((N_META, LANES), lambda i: (0, 0)),
                  pl.BlockSpec((1, LANES), lambda i: (0, 0))],
        out_specs=[pl.BlockSpec((1, s, LANES), lambda i: (i, 0, 0)),
                   pl.BlockSpec((1, N_META, LANES), lambda i: (i, 0, 0))],
        out_shape=[jax.ShapeDtypeStruct((b, s, LANES), F32), jax.ShapeDtypeStruct((b, N_META, LANES), F32)],
        compiler_params=_params("parallel"),
        name="forget_cumsum",
    )(fx, fm, bias)


def _attn_kernel(q_ref, k_ref, v_ref, km_ref, vm_ref, cx_ref, cm_ref, o_ref, *, tq):
    i = pl.program_id(2)
    q = q_ref[0]
    scale = HEAD_DIM ** -0.5

    def scores(kb, cb):
        s = lax.dot_general(q, kb, (((1,), (1,)), ((), ())), preferred_element_type=F32)
        return s * scale - cb

    def partial_softmax(s, vb):
        m = jnp.max(s, axis=-1, keepdims=True)
        p = jnp.exp(s - m)
        return m, jnp.sum(p, axis=-1, keepdims=True), jnp.dot(p.astype(BF16), vb, preferred_element_type=F32)

    def merge(a, b):
        m = jnp.maximum(a[0], b[0])
        wa = jnp.exp(a[0] - m)
        wb = jnp.exp(b[0] - m)
        return m, wa * a[1] + wb * b[1], wa * a[2] + wb * b[2]

    def update(s, vb, m, l, acc):
        m_new = jnp.maximum(m, jnp.max(s, axis=-1, keepdims=True))
        alpha = jnp.exp(m - m_new)
        p = jnp.exp(s - m_new)
        l = alpha * l + jnp.sum(p, axis=-1, keepdims=True)
        acc = alpha * acc + jnp.dot(p.astype(BF16), vb, preferred_element_type=F32)
        return m_new, l, acc

    def block(j):
        off = pl.multiple_of(j * tq, tq)
        return (scores(k_ref[0, pl.ds(off, tq), :], cx_ref[0, 0, pl.ds(j, 1), :]), v_ref[0, pl.ds(off, tq), :])

    def body(j, carry):
        s, vb = block(j)
        return update(s, vb, *carry)

    init = (jnp.full((tq, 1), -jnp.inf, F32), jnp.zeros((tq, 1), F32), jnp.zeros((tq, HEAD_DIM), F32))
    before = lax.fori_loop(0, i, body, init)

    s, vb = block(i)
    row = lax.broadcasted_iota(jnp.int32, s.shape, 0)
    col = lax.broadcasted_iota(jnp.int32, s.shape, 1)
    diag = partial_softmax(jnp.where(row >= col, s, -jnp.inf), vb)
    s = scores(km_ref[...], cm_ref[0])
    col = lax.broadcasted_iota(jnp.int32, s.shape, 1)
    meta = partial_softmax(jnp.where(col < N_META, s, -jnp.inf), vm_ref[...])
    m, l, acc = merge(before, merge(diag, meta))
    o_ref[0] = (acc / l).astype(o_ref.dtype)


def _attention(qkv, kvm, cx, cm, n_heads, tq):
    b, s, _ = qkv.shape
    nk = s // tq
    blk = lambda f: pl.BlockSpec((1, tq, HEAD_DIM), f)
    return pl.pallas_call(
        functools.partial(_attn_kernel, tq=tq),
        grid=(b, n_heads, nk),
        in_specs=[blk(lambda bi, h, i: (bi, i, h)),
                  pl.BlockSpec((1, s, HEAD_DIM), lambda bi, h, i: (bi, 0, n_heads + h)),
                  pl.BlockSpec((1, s, HEAD_DIM), lambda bi, h, i: (bi, 0, 2 * n_heads + h)),
                  pl.BlockSpec((LANES, HEAD_DIM), lambda bi, h, i: (0, n_heads + h)),
                  pl.BlockSpec((LANES, HEAD_DIM), lambda bi, h, i: (0, 2 * n_heads + h)),
                  pl.BlockSpec((1, 1, nk, tq), lambda bi, h, i: (bi, h, 0, 0)),
                  pl.BlockSpec((1, 1, LANES), lambda bi, h, i: (h, 0, 0))],
        out_specs=blk(lambda bi, h, i: (bi, i, h)),
        out_shape=jax.ShapeDtypeStruct((b, s, n_heads * HEAD_DIM), BF16),
        compiler_params=_params("parallel", "parallel", "parallel"),
        name="fox_attention",
    )(qkv, qkv, qkv, kvm, kvm, cx, cm)


def _conv_kernel(ua_ref, ug_ref, ma_ref, mg_ref, w_ref, cb_ref, g_ref, b_ref, o_ref, hb, hs, cbuf, *, ts):
    t = pl.program_id(1)
    nch = hb.shape[0]
    off = CONV_HALO - (CONV_KERNEL - 1)
    span = ts + CONV_HALO - SUBLANES
    lanes = lambda j: slice(j * LANES, (j + 1) * LANES)

    @pl.when(t == 0)
    def _():
        for j in range(nch):
            hb[j, 0:CONV_HALO - N_META, :] = jnp.zeros((CONV_HALO - N_META, LANES), F32)
            hb[j, CONV_HALO - N_META:CONV_HALO, :] = (
                ma_ref[:, lanes(j)].astype(F32) * jax.nn.sigmoid(mg_ref[:, lanes(j)].astype(F32)))

    for j in range(nch):
        hb[j, CONV_HALO:CONV_HALO + ts, :] = (
            ua_ref[0, :, lanes(j)].astype(F32) * jax.nn.sigmoid(ug_ref[0, :, lanes(j)].astype(F32)))

    def lane_chunk(j, carry):
        for s in range(1, SUBLANES):
            hs[s - 1] = hb[j, pl.ds(s, span), :]

        def row_chunk(i, c2):
            r0 = pl.multiple_of(i * CONV_ROWS, CONV_ROWS)
            tile_rows = lambda v: jnp.concatenate([v] * (CONV_ROWS // SUBLANES), axis=0)
            acc = tile_rows(cb_ref[j])
            for k in range(CONV_KERNEL):
                a, s = divmod(off + k, SUBLANES)
                rows = pl.ds(r0 + SUBLANES * a, CONV_ROWS)
                src = hb[j, rows, :] if s == 0 else hs[s - 1, rows, :]
                acc = acc + tile_rows(w_ref[j, k]) * src
            cbuf[j, pl.ds(r0, CONV_ROWS), :] = acc
            return c2

        lax.fori_loop(0, ts // CONV_ROWS, row_chunk, 0)
        return carry

    lax.fori_loop(0, nch, lane_chunk, 0)

    for j in range(nch):
        hb[j, 0:CONV_HALO, :] = hb[j, ts:ts + CONV_HALO, :]

    c = cbuf[...]
    n_ch = nch * LANES
    mu = jnp.sum(jnp.sum(c, axis=0), axis=-1, keepdims=True) / n_ch
    xc = c - mu[None]
    var = jnp.sum(jnp.sum(xc * xc, axis=0), axis=-1, keepdims=True) / n_ch
    y = xc * lax.rsqrt(var + LN_EPS)[None] * g_ref[...] + b_ref[...]
    y = y * jax.nn.sigmoid(y)
    for j in range(nch):
        o_ref[0, :, lanes(j)] = y[j].astype(o_ref.dtype)


def _conv_branch(u, um, conv_w, conv_b, ln_g, ln_b, ts):
    b, s, c2 = u.shape
    c = c2 // 2
    nch = c // LANES
    chunked = lambda v: jnp.transpose(v.astype(F32).reshape(-1, nch, LANES), (1, 0, 2))
    w_rep = jnp.broadcast_to(chunked(conv_w)[:, :, None, :], (nch, CONV_KERNEL, SUBLANES, LANES))
    cb_rep = jnp.broadcast_to(chunked(conv_b), (nch, SUBLANES, LANES))
    full = lambda shape: pl.BlockSpec(shape, lambda bi, t: (0,) * len(shape))
    return pl.pallas_call(
        functools.partial(_conv_kernel, ts=ts),
        grid=(b, s // ts),
        in_specs=[pl.BlockSpec((1, ts, c), lambda bi, t: (bi, t, 0)),
                  pl.BlockSpec((1, ts, c), lambda bi, t: (bi, t, 1)),
                  pl.BlockSpec((N_META, c), lambda bi, t: (0, 0)),
                  pl.BlockSpec((N_META, c), lambda bi, t: (0, 1)),
                  full((nch, CONV_KERNEL, SUBLANES, LANES)),
                  full((nch, SUBLANES, LANES)),
                  full((nch, 1, LANES)),
                  full((nch, 1, LANES))],
        out_specs=pl.BlockSpec((1, ts, c), lambda bi, t: (bi, t, 0)),
        out_shape=jax.ShapeDtypeStruct((b, s, c), BF16),
        scratch_shapes=[pltpu.VMEM((nch, ts + CONV_HALO, LANES), F32),
                        pltpu.VMEM((SUBLANES - 1, ts + CONV_HALO - SUBLANES, LANES), F32),
                        pltpu.VMEM((nch, ts, LANES), F32)],
        compiler_params=_params("parallel", "arbitrary"),
        name="conformer_conv",
    )(u, u, um, um, w_rep, cb_rep, chunked(ln_g), chunked(ln_b))


def _outproj_kernel(a_ref, c_ref, wa_ref, wc_ref, x_ref, o_ref, wab_ref, wcb_ref):
    @pl.when(pl.program_id(1) == 0)
    def _():
        wab_ref[...] = wa_ref[0].astype(BF16)
        wcb_ref[...] = wc_ref[0].astype(BF16)

    acc = jnp.dot(a_ref[...], wab_ref[...], preferred_element_type=F32)
    acc = acc + jnp.dot(c_ref[...], wcb_ref[...], preferred_element_type=F32)
    o_ref[...] = x_ref[...] + acc


def _out_projection(a, c, w, x, tm, tn):
    m, ka = a.shape
    kc = c.shape[1]
    n = w.shape[2]
    assert ka == kc
    return pl.pallas_call(
        _outproj_kernel,
        grid=(n // tn, m // tm),
        in_specs=[pl.BlockSpec((tm, ka), lambda j, i: (i, 0)),
                  pl.BlockSpec((tm, kc), lambda j, i: (i, 0)),
                  pl.BlockSpec((1, ka, tn), lambda j, i: (0, 0, j)),
                  pl.BlockSpec((1, kc, tn), lambda j, i: (0, 1, j)),
                  pl.BlockSpec((tm, tn), lambda j, i: (i, j))],
        out_specs=pl.BlockSpec((tm, tn), lambda j, i: (i, j)),
        out_shape=jax.ShapeDtypeStruct((m, n), F32),
        scratch_shapes=[pltpu.VMEM((ka, tn), BF16), pltpu.VMEM((kc, tn), BF16)],
        compiler_params=_params("parallel", "arbitrary"),
        name="out_projection",
    )(a, c, w, w, x)


def _router_kernel(h_ref, g_ref, wr_ref, br_ref, xp_ref, info_ref):
    h = h_ref[...]
    xn = h * lax.rsqrt(jnp.mean(h * h, axis=-1, keepdims=True) + RMS_EPS) * g_ref[...]
    half = xn.shape[1] // 2
    xp_ref[...] = _pack_bf16_pair(xn[:, :half], xn[:, half:])
    logits = jnp.dot(xn, wr_ref[...], precision=lax.Precision.HIGHEST, preferred_element_type=F32) + br_ref[...]
    lane = lax.broadcasted_iota(jnp.int32, logits.shape, 1).astype(F32)
    rmax = lambda v: jnp.max(v, axis=-1, keepdims=True)
    rmin = lambda v: jnp.min(v, axis=-1, keepdims=True)
    rsum = lambda v: jnp.sum(v, axis=-1, keepdims=True)
    far = float(LANES)

    is_g = lane < N_GROUPS
    lg = jnp.where(is_g, logits, -jnp.inf)
    mg = rmax(lg)
    g_idx = rmin(jnp.where(lg == mg, lane, far))
    g_w = 1.0 / rsum(jnp.where(is_g, jnp.exp(lg - mg), 0.0))

    lo = N_GROUPS + EXPERTS_PER_GROUP * g_idx
    in_e = jnp.logical_and(lane >= lo, lane < lo + EXPERTS_PER_GROUP)
    le = jnp.where(in_e, logits, -jnp.inf)
    pe = jnp.where(in_e, jnp.exp(le - rmax(le)), 0.0)
    prob = jnp.where(in_e, pe / rsum(pe), -1.0)
    p1 = rmax(prob)
    i1 = rmin(jnp.where(prob == p1, lane, far))
    prob2 = jnp.where(lane == i1, -1.0, prob)
    p2 = rmax(prob2)
    i2 = rmin(jnp.where(prob2 == p2, lane, far))
    den = p1 + p2
    w1 = g_w * (p1 / den)
    w2 = g_w * (p2 / den)
    info = jnp.where(lane == 0, w1, jnp.where(lane == 1, w2, jnp.where(
        lane == 2, i1 - N_GROUPS, jnp.where(lane == 3, i2 - N_GROUPS, 0.0))))
    info_ref[...] = info


def _norm_router(h, g, wr, br, tm):
    m, d = h.shape
    return pl.pallas_call(
        _router_kernel,
        grid=(m // tm,),
        in_specs=[pl.BlockSpec((tm, d), lambda i: (i, 0)),
                  pl.BlockSpec((1, d), lambda i: (0, 0)),
                  pl.BlockSpec((d, LANES), lambda i: (0, 0)),
                  pl.BlockSpec((1, LANES), lambda i: (0, 0))],
        out_specs=[pl.BlockSpec((tm, d // 2), lambda i: (i, 0)), pl.BlockSpec((tm, LANES), lambda i: (i, 0))],
        out_shape=[jax.ShapeDtypeStruct((m, d // 2), U32), jax.ShapeDtypeStruct((m, LANES), F32)],
        compiler_params=_params("parallel"),
        name="norm_router",
    )(h, g.reshape(1, d).astype(F32), wr, br)


def _moe_kernel(te_ref, tn_ref, tb_ref, pos_ref, xp_hbm, wg_ref, wu_ref, wd_ref, y_hbm,
                xq, xb, gacc, uacc, wd0, ystage, asg_ref, pend, gsem, ssem, *, y_tail_rows):
    t = pl.program_id(0)
    f = pl.program_id(1)
    n_tiles = pl.num_programs(0)
    n = tn_ref[t]
    tm, half = ystage.shape
    groups = lambda cnt: (cnt + DMA_GROUP - 1) // DMA_GROUP

    def gather_copy(r, tok):
        return pltpu.make_async_copy(xp_hbm.at[pl.ds(tok, 1), :], xq.at[pl.ds(r, 1), :], gsem)

    def start_gather(tile):
        base = tb_ref[tile]
        cnt = tn_ref[tile]

        def body(g, carry):
            for u in range(DMA_GROUP):
                r = g * DMA_GROUP + u
                gather_copy(r, asg_ref[base + jnp.minimum(r, cnt - 1)] >> 1).start()
            return carry
        lax.fori_loop(0, groups(cnt), body, 0)

    def wait_gather(cnt):
        def body(g, carry):
            for u in range(DMA_GROUP):
                gather_copy(g * DMA_GROUP + u, 0).wait()
            return carry
        lax.fori_loop(0, groups(cnt), body, 0)

    n_granules = (n + MOE_ROW_GRANULE - 1) // MOE_ROW_GRANULE
    row_counts = [k * MOE_ROW_GRANULE for k in range(1, tm // MOE_ROW_GRANULE + 1)]

    def out_copy(m, base):
        return pltpu.make_async_copy(ystage.at[pl.ds(0, m), :], y_hbm.at[pl.ds(base, m), :], ssem)

    def start_scatter(m):
        out_copy(m, pl.multiple_of(tb_ref[t], SUBLANES)).start()
        pend[0] = m

    def wait_scatter():
        for m in row_counts:
            @pl.when(pend[0] == m)
            def _(m=m):
                out_copy(m, 0).wait()
        pend[0] = 0

    @pl.when(jnp.logical_and(t == 0, f == 0))
    def _():
        xq[...] = jnp.zeros(xq.shape, xq.dtype)
        pend[0] = 0
        for part in range(y_tail_rows // tm):
            fill = pltpu.make_async_copy(xq, y_hbm.at[pl.ds(y_hbm.shape[0] - (part + 1) * tm, tm), :], ssem)
            fill.start()
            fill.wait()

        def invert(g, carry):
            for u in range(DMA_GROUP):
                a = g * DMA_GROUP + u
                asg_ref[pos_ref[a]] = a
            return carry
        lax.fori_loop(0, pos_ref.shape[0] // DMA_GROUP, invert, 0)
        start_gather(0)

    def unpack_rows(m):
        lo, hi = _unpack_bf16_pair(xq[0:m, :])
        xb[0, 0:m, :] = lo.astype(BF16)
        xb[1, 0:m, :] = hi.astype(BF16)

    @pl.when(jnp.logical_and(f == 0, n > 0))
    def _():
        wait_gather(n)
        for m in row_counts:
            pl.when(n_granules * MOE_ROW_GRANULE == m)(functools.partial(unpack_rows, m))
        wd0[...] = wd_ref[0].astype(BF16)

    @pl.when(jnp.logical_and(f == 0, t + 1 < n_tiles))
    def _():
        start_gather(t + 1)

    def ffn_rows(m):
        x = xb[f, 0:m, :]
        g_part = jnp.dot(x, wg_ref[0].astype(BF16), preferred_element_type=F32)
        u_part = jnp.dot(x, wu_ref[0].astype(BF16), preferred_element_type=F32)

        @pl.when(f == 0)
        def _():
            gacc[0:m, :] = g_part
            uacc[0:m, :] = u_part

        @pl.when(f == 1)
        def _():
            g = gacc[0:m, :] + g_part
            u = uacc[0:m, :] + u_part
            hid = ((g * jax.nn.sigmoid(g)) * u).astype(BF16)
            fh = wd0.shape[0]
            out = jnp.dot(hid[:, :fh], wd0[...], preferred_element_type=F32)
            out = out + jnp.dot(hid[:, fh:], wd_ref[0].astype(BF16), preferred_element_type=F32)
            wait_scatter()
            ystage[0:m, :] = _pack_bf16_pair(out[:, :half], out[:, half:])
            start_scatter(m)

    for m in row_counts:
        pl.when(n_granules * MOE_ROW_GRANULE == m)(functools.partial(ffn_rows, m))

    @pl.when(jnp.logical_and(f == 1, t == n_tiles - 1))
    def _():
        wait_scatter()


def _moe(xp, tile_e, tile_n, tile_base, pos, n_sorted, w_gate, w_up, w_down, tm):
    assert tm % MOE_ROW_GRANULE == 0 and MOE_ROW_GRANULE % DMA_GROUP == 0 and pos.shape[0] % DMA_GROUP == 0
    t_rows, half = xp.shape
    d = 2 * half
    n_tiles = tile_e.shape[0]
    ff = w_gate.shape[3]
    fh = ff // 2
    y_rows = n_sorted + tm
    y_tail_rows = -(-(y_rows - 2 * t_rows) // tm) * tm

    def fidx(f, tn, t):
        return jnp.where(tn[t] > 0, f, 1)

    grid_spec = pltpu.PrefetchScalarGridSpec(
        num_scalar_prefetch=4,
        grid=(n_tiles, 2),
        in_specs=[pl.BlockSpec(memory_space=pl.ANY),
                  pl.BlockSpec((1, 1, half, ff), lambda t, f, te, tn, tb, ds: (0, te[t], fidx(f, tn, t), 0)),
                  pl.BlockSpec((1, 1, half, ff), lambda t, f, te, tn, tb, ds: (0, te[t], fidx(f, tn, t), 0)),
                  pl.BlockSpec((1, 1, fh, d), lambda t, f, te, tn, tb, ds: (0, te[t], fidx(f, tn, t), 0))],
        out_specs=pl.BlockSpec(memory_space=pl.ANY),
        scratch_shapes=[pltpu.VMEM((tm, half), U32), pltpu.VMEM((2, tm, half), BF16),
                        pltpu.VMEM((tm, ff), F32), pltpu.VMEM((tm, ff), F32), pltpu.VMEM((fh, d), BF16),
                        pltpu.VMEM((tm, half), U32), pltpu.SMEM((n_sorted,), jnp.int32), pltpu.SMEM((1,), jnp.int32),
                        pltpu.SemaphoreType.DMA(()), pltpu.SemaphoreType.DMA(())],
    )
    return pl.pallas_call(
        functools.partial(_moe_kernel_4d, y_tail_rows=y_tail_rows),
        grid_spec=grid_spec,
        out_shape=jax.ShapeDtypeStruct((y_rows, half), U32),
        compiler_params=_params("arbitrary", "arbitrary"),
        name="sparse_moe",
    )(tile_e, tile_n, tile_base, pos, xp, w_gate, w_up, w_down)


def _moe_kernel_4d(te_ref, tn_ref, tb_ref, pos_ref, xp_hbm, wg_ref, wu_ref, wd_ref, *rest, y_tail_rows):
    _moe_kernel(te_ref, tn_ref, tb_ref, pos_ref, xp_hbm, wg_ref.at[0], wu_ref.at[0], wd_ref.at[0], *rest,
                y_tail_rows=y_tail_rows)


def _routing_tables(e_idx, tm, n_tiles):
    flat_e = e_idx.reshape(-1)
    n_assign = flat_e.shape[0]
    onehot = (flat_e[:, None] == jnp.arange(N_EXPERTS, dtype=jnp.int32)[None, :]).astype(jnp.int32)
    csum = jnp.cumsum(onehot, axis=0)
    rank = jnp.sum(csum * onehot, axis=1) - 1
    counts = csum[-1]
    seg = (counts + SUBLANES - 1) // SUBLANES * SUBLANES
    row_end = jnp.cumsum(seg)
    row_start = row_end - seg
    tiles_per_e = (counts + tm - 1) // tm
    tile_end = jnp.cumsum(tiles_per_e)
    tile_start = tile_end - tiles_per_e
    total = tile_end[-1]
    pos = jnp.sum(onehot * row_start[None, :], axis=1) + rank
    n_sorted = n_assign + N_EXPERTS * SUBLANES
    tid = jnp.arange(n_tiles, dtype=jnp.int32)
    te = jnp.sum((tid[:, None] >= tile_end[None, :]).astype(jnp.int32), axis=1)
    te = jnp.minimum(te, N_EXPERTS - 1)
    active = tid < total
    first = (tid - tile_start[te]) * tm
    tn = jnp.where(active, jnp.clip(counts[te] - first, 0, tm), 0).astype(jnp.int32)
    tb = jnp.where(active, row_start[te] + first, 0).astype(jnp.int32)
    last_e = te[jnp.maximum(total - 1, 0)]
    te = jnp.where(active, te, last_e).astype(jnp.int32)
    return te, tn, tb, pos.astype(jnp.int32), n_sorted


def _final_kernel(pos_ref, h_ref, info_ref, g_ref, y_hbm, o_ref, ybuf, sem):
    i = pl.program_id(0)
    tm, half = ybuf.shape[2:]
    slot = lax.rem(i, 2)

    def row_copy(buf, k, r, src):
        return pltpu.make_async_copy(y_hbm.at[pl.ds(src, 1), :], ybuf.at[buf, k, pl.ds(r, 1), :], sem.at[buf])

    def start_gather(step, buf):
        def body(g, carry):
            for u in range(DMA_GROUP):
                r = g * DMA_GROUP + u
                for k in range(2):
                    row_copy(buf, k, r, pos_ref[2 * (step * tm + r) + k]).start()
            return carry
        lax.fori_loop(0, tm // DMA_GROUP, body, 0)

    def wait_gather(buf):
        def body(g, carry):
            for u in range(DMA_GROUP):
                for k in range(2):
                    row_copy(buf, k, g * DMA_GROUP + u, 0).wait()
            return carry
        lax.fori_loop(0, tm // DMA_GROUP, body, 0)

    @pl.when(i == 0)
    def _():
        start_gather(0, 0)

    @pl.when(i + 1 < pl.num_programs(0))
    def _():
        start_gather(i + 1, 1 - slot)

    wait_gather(slot)
    w = info_ref[...]
    w0 = w[:, 0:1]
    w1 = w[:, 1:2]
    y0_lo, y0_hi = _unpack_bf16_pair(ybuf[slot, 0])
    y1_lo, y1_hi = _unpack_bf16_pair(ybuf[slot, 1])
    h_lo = h_ref[:, :half] + w0 * y0_lo + w1 * y1_lo
    h_hi = h_ref[:, half:] + w0 * y0_hi + w1 * y1_hi
    ms = (jnp.sum(h_lo * h_lo, axis=-1, keepdims=True) + jnp.sum(h_hi * h_hi, axis=-1, keepdims=True)) / (2 * half)
    r = lax.rsqrt(ms + RMS_EPS)
    o_ref[:, :half] = h_lo * r * g_ref[:, :half]
    o_ref[:, half:] = h_hi * r * g_ref[:, half:]


def _combine_final(h, y, pos, info, g, tm):
    m, d = h.shape
    assert tm % DMA_GROUP == 0
    grid_spec = pltpu.PrefetchScalarGridSpec(
        num_scalar_prefetch=1,
        grid=(m // tm,),
        in_specs=[pl.BlockSpec((tm, d), lambda i, p: (i, 0)),
                  pl.BlockSpec((tm, LANES), lambda i, p: (i, 0)),
                  pl.BlockSpec((1, d), lambda i, p: (0, 0)),
                  pl.BlockSpec(memory_space=pl.ANY)],
        out_specs=pl.BlockSpec((tm, d), lambda i, p: (i, 0)),
        scratch_shapes=[pltpu.VMEM((2, 2, tm, d // 2), U32), pltpu.SemaphoreType.DMA((2,))],
    )
    return pl.pallas_call(
        _final_kernel,
        grid_spec=grid_spec,
        out_shape=jax.ShapeDtypeStruct((m, d), F32),
        compiler_params=_params("arbitrary"),
        name="combine_final",
    )(pos, h, info, g.reshape(1, d).astype(F32), y)


def kernel(x, meta_tokens, attn_norm_g, w_in, forget_bias, conv_w, conv_b, conv_ln_g, conv_ln_b, w_out,
           ffn_norm_g, w_router_group, b_router_group, w_router_expert, b_router_expert,
           w_expert_gate, w_expert_up, w_expert_down, final_norm_g):
    b, s, d = x.shape
    depth = w_in.shape[0]
    assert depth == 1, "one layer: the meta rows are not carried past the mixer"
    n_heads = forget_bias.shape[1]
    aw = n_heads * HEAD_DIM
    cw = conv_w.shape[2]
    t_rows = b * s
    x2 = x.reshape(t_rows, d)

    xn = _rmsnorm(x2, attn_norm_g[0], BF16, tm=512)
    mn = _rmsnorm(meta_tokens.astype(x.dtype), attn_norm_g[0], BF16, tm=N_META)

    w_t = jnp.transpose(w_in[0])
    qkv, qkv_m = _projection(xn, mn, w_t, 0, 3 * aw, BF16, 1024, 512, "in_proj_qkv")
    fx, f_m = _projection(xn, mn, w_t, 3 * aw, LANES, F32, 1024, LANES, "in_proj_gate")
    u, u_m = _projection(xn, mn, w_t, 3 * aw + n_heads, 2 * cw, BF16, 1024, 512, "in_proj_glu")

    tq = 512
    bias_f = jnp.zeros((1, LANES), F32).at[0, :n_heads].set(forget_bias[0].astype(F32))
    cx, cm = _forget_cumsum(fx.reshape(b, s, LANES), f_m, bias_f)
    cx = jnp.transpose(cx[:, :, :n_heads], (0, 2, 1)).reshape(b, n_heads, s // tq, tq)
    cm = jnp.zeros((n_heads, 1, LANES), F32).at[:, 0, :N_META].set(jnp.transpose(cm[0, :, :n_heads]))
    kvm = jnp.zeros((LANES, 3 * aw), BF16).at[:N_META].set(qkv_m)
    attn = _attention(qkv.reshape(b, s, 3 * aw), kvm, cx, cm, n_heads, tq)

    conv = _conv_branch(u.reshape(b, s, 2 * cw), u_m, conv_w[0], conv_b[0], conv_ln_g[0], conv_ln_b[0], ts=256)

    h1 = _out_projection(attn.reshape(t_rows, aw), conv.reshape(t_rows, cw), w_out, x2, 1024, 512)

    n_r = N_GROUPS + N_EXPERTS
    wr = jnp.zeros((d, LANES), F32).at[:, :N_GROUPS].set(w_router_group[0]).at[:, N_GROUPS:n_r].set(w_router_expert[0])
    br = jnp.zeros((1, LANES), F32).at[0, :N_GROUPS].set(b_router_group[0]).at[0, N_GROUPS:n_r].set(b_router_expert[0])
    xp, info = _norm_router(h1, ffn_norm_g[0], wr, br, tm=256)

    tm_moe = 512
    n_tiles = (2 * t_rows) // tm_moe + N_EXPERTS
    te, tn, tb, pos, n_sorted = _routing_tables(info[:, 2:4].astype(jnp.int32), tm_moe, n_tiles)
    y = _moe(xp, te, tn, tb, pos, n_sorted, w_expert_gate, w_expert_up, w_expert_down, tm_moe)

    out = _combine_final(h1, y, pos, info, final_norm_g, tm=256)
    return out.reshape(b, s, d)
```

```python
import functools

import jax
import jax.numpy as jnp
from jax import lax
from jax.experimental import pallas as pl
from jax.experimental.pallas import tpu as pltpu

F32 = jnp.float32
BF16 = jnp.bfloat16
U32 = jnp.uint32

N_META = 16
HEAD_DIM = 128
CONV_KERNEL = 31
N_GROUPS = 8
EXPERTS_PER_GROUP = 8
N_EXPERTS = N_GROUPS * EXPERTS_PER_GROUP
RMS_EPS = 1e-6
LN_EPS = 1e-5

LANES = 128
SUBLANES = 8
VMEM_LIMIT = 56 * 1024 * 1024
CONV_HALO = 32
CONV_ROWS = 256
MOE_ROW_GRANULE = 64
DMA_GROUP = 8


def _params(*sem):
    return pltpu.CompilerParams(dimension_semantics=sem, vmem_limit_bytes=VMEM_LIMIT)


def _pack_bf16_pair(lo, hi):
    lo_bits = lax.bitcast_convert_type(lo.astype(BF16).astype(F32), U32)
    hi_bits = lax.bitcast_convert_type(hi.astype(BF16).astype(F32), U32)
    return (lo_bits >> 16) | hi_bits


def _unpack_bf16_pair(p):
    lo = lax.bitcast_convert_type(p << 16, F32)
    hi = lax.bitcast_convert_type(p & jnp.uint32(0xFFFF0000), F32)
    return lo, hi


def _rmsnorm_kernel(x_ref, g_ref, o_ref):
    x = x_ref[...].astype(F32)
    y = x * lax.rsqrt(jnp.mean(x * x, axis=-1, keepdims=True) + RMS_EPS)
    o_ref[...] = (y * g_ref[...]).astype(o_ref.dtype)


def _rmsnorm(x, g, out_dtype, tm):
    m, d = x.shape
    return pl.pallas_call(
        _rmsnorm_kernel,
        grid=(m // tm,),
        in_specs=[pl.BlockSpec((tm, d), lambda i: (i, 0)), pl.BlockSpec((1, d), lambda i: (0, 0))],
        out_specs=pl.BlockSpec((tm, d), lambda i: (i, 0)),
        out_shape=jax.ShapeDtypeStruct((m, d), out_dtype),
        compiler_params=_params("parallel"),
        name="rmsnorm",
    )(x, g.reshape(1, d).astype(F32))


def _proj_kernel(a_ref, am_ref, wt_ref, o_ref, om_ref, wb_ref):
    i = pl.program_id(1)
    tn, k = wt_ref.shape
    kc = min(k, 512)

    @pl.when(i == 0)
    def _():
        for c in range(k // kc):
            wb_ref[c * kc:(c + 1) * kc, :] = wt_ref[:, c * kc:(c + 1) * kc].T.astype(BF16)
        om_ref[...] = jnp.dot(am_ref[...], wb_ref[...], preferred_element_type=F32).astype(om_ref.dtype)

    o_ref[...] = jnp.dot(a_ref[...], wb_ref[...], preferred_element_type=F32).astype(o_ref.dtype)


def _projection(a, am, wt, row0, n_cols, out_dtype, tm, tn, name):
    m, k = a.shape
    mm = am.shape[0]
    assert n_cols % tn == 0 and m % tm == 0 and row0 % SUBLANES == 0
    if row0 % tn == 0:
        w_spec = pl.BlockSpec((tn, k), lambda j, i: (row0 // tn + j, 0))
    else:
        w_spec = pl.BlockSpec((pl.Element(tn), pl.Element(k)), lambda j, i: ((row0 // SUBLANES + j * (tn // SUBLANES)) * SUBLANES, 0))
    return pl.pallas_call(
        _proj_kernel,
        grid=(n_cols // tn, m // tm),
        in_specs=[pl.BlockSpec((tm, k), lambda j, i: (i, 0)),
                  pl.BlockSpec((mm, k), lambda j, i: (0, 0)),
                  w_spec],
        out_specs=[pl.BlockSpec((tm, tn), lambda j, i: (i, j)), pl.BlockSpec((mm, tn), lambda j, i: (0, j))],
        out_shape=[jax.ShapeDtypeStruct((m, n_cols), out_dtype), jax.ShapeDtypeStruct((mm, n_cols), out_dtype)],
        scratch_shapes=[pltpu.VMEM((k, tn), BF16)],
        compiler_params=_params("parallel", "arbitrary"),
        name=name,
    )(a, am, wt)


def _log_sigmoid(z):
    return jnp.minimum(z, 0.0) - jnp.log1p(jnp.exp(-jnp.abs(z)))


def _split3_bf16(x):
    hi = x.astype(BF16)
    r1 = x - hi.astype(F32)
    mid = r1.astype(BF16)
    lo = (r1 - mid.astype(F32)).astype(BF16)
    return hi, mid, lo


def _cumsum_rows(l, tril):
    dot = functools.partial(jnp.dot, preferred_element_type=F32)
    hi, mid, lo = _split3_bf16(l)
    return dot(tril, hi) + dot(tril, mid) + dot(tril, lo)


def _tril(n):
    r = lax.broadcasted_iota(jnp.int32, (n, n), 0)
    c = lax.broadcasted_iota(jnp.int32, (n, n), 1)
    return jnp.where(r >= c, 1.0, 0.0).astype(BF16)


def _cumsum_kernel(fx_ref, fm_ref, b_ref, cx_ref, cm_ref, *, blk):
    bias = b_ref[...]
    cm = _cumsum_rows(_log_sigmoid(fm_ref[...] + bias), _tril(N_META))
    cm_ref[0] = cm
    carry = cm[N_META - 1:N_META, :]
    tril = _tril(blk)
    for j in range(fx_ref.shape[1] // blk):
        l = _log_sigmoid(fx_ref[0, j * blk:(j + 1) * blk, :] + bias)
        c = _cumsum_rows(l, tril) + carry
        cx_ref[0, j * blk:(j + 1) * blk, :] = c
        carry = c[blk - 1:blk, :]


def _forget_cumsum(fx, fm, bias):
    b, s, _ = fx.shape
    return pl.pallas_call(
        functools.partial(_cumsum_kernel, blk=256),
        grid=(b,),
        in_specs=[pl.BlockSpec((1, s, LANES), lambda i: (i, 0, 0)),
                  pl.BlockSpec((N_META, LANES), lambda i: (0, 0)),
                  pl.BlockSpec((1, LANES), lambda i: (0, 0))],
        out_specs=[pl.BlockSpec((1, s, LANES), lambda i: (i, 0, 0)),
                   pl.BlockSpec((1, N_META, LANES), lambda i: (i, 0, 0))],
        out_shape=[jax.ShapeDtypeStruct((b, s, LANES), F32), jax.ShapeDtypeStruct((b, N_META, LANES), F32)],
        compiler_params=_params("parallel"),
        name="forget_cumsum",
    )(fx, fm, bias)


def _attn_kernel(q_ref, k_ref, v_ref, km_ref, vm_ref, cx_ref, cm_ref, o_ref, *, tq):
    i = pl.program_id(2)
    q = q_ref[0]
    scale = HEAD_DIM ** -0.5

    def scores(kb, cb):
        s = lax.dot_general(q, kb, (((1,), (1,)), ((), ())), preferred_element_type=F32)
        return s * scale - cb

    def partial_softmax(s, vb):
        m = jnp.max(s, axis=-1, keepdims=True)
        p = jnp.exp(s - m)
        return m, jnp.sum(p, axis=-1, keepdims=True), jnp.dot(p.astype(BF16), vb, preferred_element_type=F32)

    def merge(a, b):
        m = jnp.maximum(a[0], b[0])
        wa = jnp.exp(a[0] - m)
        wb = jnp.exp(b[0] - m)
        return m, wa * a[1] + wb * b[1], wa * a[2] + wb * b[2]

    def update(s, vb, m, l, acc):
        m_new = jnp.maximum(m, jnp.max(s, axis=-1, keepdims=True))
        alpha = jnp.exp(m - m_new)
        p = jnp.exp(s - m_new)
        l = alpha * l + jnp.sum(p, axis=-1, keepdims=True)
        acc = alpha * acc + jnp.dot(p.astype(BF16), vb, preferred_element_type=F32)
        return m_new, l, acc

    def block(j):
        off = pl.multiple_of(j * tq, tq)
        return (scores(k_ref[0, pl.ds(off, tq), :], cx_ref[0, 0, pl.ds(j, 1), :]), v_ref[0, pl.ds(off, tq), :])

    def body(j, carry):
        s, vb = block(j)
        return update(s, vb, *carry)

    init = (jnp.full((tq, 1), -jnp.inf, F32), jnp.zeros((tq, 1), F32), jnp.zeros((tq, HEAD_DIM), F32))
    before = lax.fori_loop(0, i, body, init)

    s, vb = block(i)
    row = lax.broadcasted_iota(jnp.int32, s.shape, 0)
    col = lax.broadcasted_iota(jnp.int32, s.shape, 1)
    diag = partial_softmax(jnp.where(row >= col, s, -jnp.inf), vb)
    s = scores(km_ref[...], cm_ref[0])
    col = lax.broadcasted_iota(jnp.int32, s.shape, 1)
    meta = partial_softmax(jnp.where(col < N_META, s, -jnp.inf), vm_ref[...])
    m, l, acc = merge(before, merge(diag, meta))
    o_ref[0] = (acc / l).astype(o_ref.dtype)


def _attention(qkv, kvm, cx, cm, n_heads, tq):
    b, s, _ = qkv.shape
    nk = s // tq
    blk = lambda f: pl.BlockSpec((1, tq, HEAD_DIM), f)
    return pl.pallas_call(
        functools.partial(_attn_kernel, tq=tq),
        grid=(b, n_heads, nk),
        in_specs=[blk(lambda bi, h, i: (bi, i, h)),
                  pl.BlockSpec((1, s, HEAD_DIM), lambda bi, h, i: (bi, 0, n_heads + h)),
                  pl.BlockSpec((1, s, HEAD_DIM), lambda bi, h, i: (bi, 0, 2 * n_heads + h)),
                  pl.BlockSpec((LANES, HEAD_DIM), lambda bi, h, i: (0, n_heads + h)),
                  pl.BlockSpec((LANES, HEAD_DIM), lambda bi, h, i: (0, 2 * n_heads + h)),
                  pl.BlockSpec((1, 1, nk, tq), lambda bi, h, i: (bi, h, 0, 0)),
                  pl.BlockSpec((1, 1, LANES), lambda bi, h, i: (h, 0, 0))],
        out_specs=blk(lambda bi, h, i: (bi, i, h)),
        out_shape=jax.ShapeDtypeStruct((b, s, n_heads * HEAD_DIM), BF16),
        compiler_params=_params("parallel", "parallel", "parallel"),
        name="fox_attention",
    )(qkv, qkv, qkv, kvm, kvm, cx, cm)


def _sigmoid(x):
    return 0.5 * jnp.tanh(0.5 * x) + 0.5


def _conv_kernel(ua_ref, ug_ref, ma_ref, mg_ref, w_ref, cb_ref, g_ref, b_ref, o_ref, hb, hs, cbuf, *, ts):
    t = pl.program_id(1)
    nch = hb.shape[0]
    off = CONV_HALO - (CONV_KERNEL - 1)
    span = ts + CONV_HALO - SUBLANES
    lanes = lambda j: slice(j * LANES, (j + 1) * LANES)

    @pl.when(t == 0)
    def _():
        for j in range(nch):
            hb[j, 0:CONV_HALO - N_META, :] = jnp.zeros((CONV_HALO - N_META, LANES), F32)
            hb[j, CONV_HALO - N_META:CONV_HALO, :] = (
                ma_ref[:, lanes(j)].astype(F32) * _sigmoid(mg_ref[:, lanes(j)].astype(F32)))

    for j in range(nch):
        hb[j, CONV_HALO:CONV_HALO + ts, :] = (
            ua_ref[0, :, lanes(j)].astype(F32) * _sigmoid(ug_ref[0, :, lanes(j)].astype(F32)))

    def lane_chunk(j, carry):
        for s in range(1, SUBLANES):
            hs[s - 1] = hb[j, pl.ds(s, span), :]

        def row_chunk(i, c2):
            r0 = pl.multiple_of(i * CONV_ROWS, CONV_ROWS)
            tile_rows = lambda v: jnp.concatenate([v] * (CONV_ROWS // SUBLANES), axis=0)
            acc = tile_rows(cb_ref[j])
            for k in range(CONV_KERNEL):
                a, s = divmod(off + k, SUBLANES)
                rows = pl.ds(r0 + SUBLANES * a, CONV_ROWS)
                src = hb[j, rows, :] if s == 0 else hs[s - 1, rows, :]
                acc = acc + tile_rows(w_ref[j, k]) * src
            cbuf[j, pl.ds(r0, CONV_ROWS), :] = acc
            return c2

        lax.fori_loop(0, ts // CONV_ROWS, row_chunk, 0)
        return carry

    lax.fori_loop(0, nch, lane_chunk, 0)

    for j in range(nch):
        hb[j, 0:CONV_HALO, :] = hb[j, ts:ts + CONV_HALO, :]

    c = cbuf[...]
    n_ch = nch * LANES
    mu = jnp.sum(jnp.sum(c, axis=0), axis=-1, keepdims=True) / n_ch
    xc = c - mu[None]
    var = jnp.sum(jnp.sum(xc * xc, axis=0), axis=-1, keepdims=True) / n_ch
    y = xc * lax.rsqrt(var + LN_EPS)[None] * g_ref[...] + b_ref[...]
    y = y * _sigmoid(y)
    for j in range(nch):
        o_ref[0, :, lanes(j)] = y[j].astype(o_ref.dtype)


def _conv_branch(u, um, conv_w, conv_b, ln_g, ln_b, ts):
    b, s, c2 = u.shape
    c = c2 // 2
    nch = c // LANES
    chunked = lambda v: jnp.transpose(v.astype(F32).reshape(-1, nch, LANES), (1, 0, 2))
    w_rep = jnp.broadcast_to(chunked(conv_w)[:, :, None, :], (nch, CONV_KERNEL, SUBLANES, LANES))
    cb_rep = jnp.broadcast_to(chunked(conv_b), (nch, SUBLANES, LANES))
    full = lambda shape: pl.BlockSpec(shape, lambda bi, t: (0,) * len(shape))
    return pl.pallas_call(
        functools.partial(_conv_kernel, ts=ts),
        grid=(b, s // ts),
        in_specs=[pl.BlockSpec((1, ts, c), lambda bi, t: (bi, t, 0)),
                  pl.BlockSpec((1, ts, c), lambda bi, t: (bi, t, 1)),
                  pl.BlockSpec((N_META, c), lambda bi, t: (0, 0)),
                  pl.BlockSpec((N_META, c), lambda bi, t: (0, 1)),
                  full((nch, CONV_KERNEL, SUBLANES, LANES)),
                  full((nch, SUBLANES, LANES)),
                  full((nch, 1, LANES)),
                  full((nch, 1, LANES))],
        out_specs=pl.BlockSpec((1, ts, c), lambda bi, t: (bi, t, 0)),
        out_shape=jax.ShapeDtypeStruct((b, s, c), BF16),
        scratch_shapes=[pltpu.VMEM((nch, ts + CONV_HALO, LANES), F32),
                        pltpu.VMEM((SUBLANES - 1, ts + CONV_HALO - SUBLANES, LANES), F32),
                        pltpu.VMEM((nch, ts, LANES), F32)],
        compiler_params=_params("parallel", "arbitrary"),
        name="conformer_conv",
    )(u, u, um, um, w_rep, cb_rep, chunked(ln_g), chunked(ln_b))


def _outproj_kernel(a_ref, c_ref, wa_ref, wc_ref, x_ref, o_ref, wab_ref, wcb_ref):
    @pl.when(pl.program_id(1) == 0)
    def _():
        wab_ref[...] = wa_ref[0].astype(BF16)
        wcb_ref[...] = wc_ref[0].astype(BF16)

    acc = jnp.dot(a_ref[...], wab_ref[...], preferred_element_type=F32)
    acc = acc + jnp.dot(c_ref[...], wcb_ref[...], preferred_element_type=F32)
    o_ref[...] = x_ref[...] + acc


def _out_projection(a, c, w, x, tm, tn):
    m, ka = a.shape
    kc = c.shape[1]
    n = w.shape[2]
    assert ka == kc
    return pl.pallas_call(
        _outproj_kernel,
        grid=(n // tn, m // tm),
        in_specs=[pl.BlockSpec((tm, ka), lambda j, i: (i, 0)),
                  pl.BlockSpec((tm, kc), lambda j, i: (i, 0)),
                  pl.BlockSpec((1, ka, tn), lambda j, i: (0, 0, j)),
                  pl.BlockSpec((1, kc, tn), lambda j, i: (0, 1, j)),
                  pl.BlockSpec((tm, tn), lambda j, i: (i, j))],
        out_specs=pl.BlockSpec((tm, tn), lambda j, i: (i, j)),
        out_shape=jax.ShapeDtypeStruct((m, n), F32),
        scratch_shapes=[pltpu.VMEM((ka, tn), BF16), pltpu.VMEM((kc, tn), BF16)],
        compiler_params=_params("parallel", "arbitrary"),
        name="out_projection",
    )(a, c, w, w, x)


def _router_kernel(h_ref, g_ref, wr_ref, br_ref, xp_ref, info_ref):
    h = h_ref[...]
    xn = h * lax.rsqrt(jnp.mean(h * h, axis=-1, keepdims=True) + RMS_EPS) * g_ref[...]
    half = xn.shape[1] // 2
    xp_ref[...] = _pack_bf16_pair(xn[:, :half], xn[:, half:])
    logits = jnp.dot(xn, wr_ref[...], precision=lax.Precision.HIGHEST, preferred_element_type=F32) + br_ref[...]
    lane = lax.broadcasted_iota(jnp.int32, logits.shape, 1).astype(F32)
    rmax = lambda v: jnp.max(v, axis=-1, keepdims=True)
    rmin = lambda v: jnp.min(v, axis=-1, keepdims=True)
    rsum = lambda v: jnp.sum(v, axis=-1, keepdims=True)
    far = float(LANES)

    is_g = lane < N_GROUPS
    lg = jnp.where(is_g, logits, -jnp.inf)
    mg = rmax(lg)
    g_idx = rmin(jnp.where(lg == mg, lane, far))
    g_w = 1.0 / rsum(jnp.where(is_g, jnp.exp(lg - mg), 0.0))

    lo = N_GROUPS + EXPERTS_PER_GROUP * g_idx
    in_e = jnp.logical_and(lane >= lo, lane < lo + EXPERTS_PER_GROUP)
    le = jnp.where(in_e, logits, -jnp.inf)
    pe = jnp.where(in_e, jnp.exp(le - rmax(le)), 0.0)
    prob = jnp.where(in_e, pe / rsum(pe), -1.0)
    p1 = rmax(prob)
    i1 = rmin(jnp.where(prob == p1, lane, far))
    prob2 = jnp.where(lane == i1, -1.0, prob)
    p2 = rmax(prob2)
    i2 = rmin(jnp.where(prob2 == p2, lane, far))
    den = p1 + p2
    w1 = g_w * (p1 / den)
    w2 = g_w * (p2 / den)
    info = jnp.where(lane == 0, w1, jnp.where(lane == 1, w2, jnp.where(
        lane == 2, i1 - N_GROUPS, jnp.where(lane == 3, i2 - N_GROUPS, 0.0))))
    info_ref[...] = info


def _norm_router(h, g, wr, br, tm):
    m, d = h.shape
    return pl.pallas_call(
        _router_kernel,
        grid=(m // tm,),
        in_specs=[pl.BlockSpec((tm, d), lambda i: (i, 0)),
                  pl.BlockSpec((1, d), lambda i: (0, 0)),
                  pl.BlockSpec((d, LANES), lambda i: (0, 0)),
                  pl.BlockSpec((1, LANES), lambda i: (0, 0))],
        out_specs=[pl.BlockSpec((tm, d // 2), lambda i: (i, 0)), pl.BlockSpec((tm, LANES), lambda i: (i, 0))],
        out_shape=[jax.ShapeDtypeStruct((m, d // 2), U32), jax.ShapeDtypeStruct((m, LANES), F32)],
        compiler_params=_params("parallel"),
        name="norm_router",
    )(h, g.reshape(1, d).astype(F32), wr, br)


def _moe_kernel(te_ref, tn_ref, tb_ref, pos_ref, xp_hbm, wg_ref, wu_ref, wd_ref, y_hbm,
                xq, xb, gacc, uacc, wd0, ystage, asg_ref, pend, gsem, ssem, *, y_tail_rows):
    t = pl.program_id(0)
    f = pl.program_id(1)
    n_tiles = pl.num_programs(0)
    n = tn_ref[t]
    tm, half = ystage.shape
    groups = lambda cnt: (cnt + DMA_GROUP - 1) // DMA_GROUP

    def gather_copy(g, u, tok):
        return pltpu.make_async_copy(xp_hbm.at[pl.ds(tok, 1), :], xq.at[g, pl.ds(u, 1), :], gsem)

    def start_gather(tile):
        base = tb_ref[tile]
        cnt = tn_ref[tile]

        def body(g, carry):
            for u in range(DMA_GROUP):
                r = g * DMA_GROUP + u
                gather_copy(g, u, asg_ref[base + jnp.minimum(r, cnt - 1)] >> 1).start()
            return carry
        lax.fori_loop(0, groups(cnt), body, 0)

    def wait_gather(cnt):
        def body(g, carry):
            for u in range(DMA_GROUP):
                gather_copy(g, u, 0).wait()
            return carry
        lax.fori_loop(0, groups(cnt), body, 0)

    n_granules = (n + MOE_ROW_GRANULE - 1) // MOE_ROW_GRANULE
    row_counts = [k * MOE_ROW_GRANULE for k in range(1, tm // MOE_ROW_GRANULE + 1)]

    def out_copy(m, base):
        return pltpu.make_async_copy(ystage.at[pl.ds(0, m), :], y_hbm.at[pl.ds(base, m), :], ssem)

    def start_scatter(m):
        out_copy(m, pl.multiple_of(tb_ref[t], SUBLANES)).start()
        pend[0] = m

    def wait_scatter():
        for m in row_counts:
            @pl.when(pend[0] == m)
            def _(m=m):
                out_copy(m, 0).wait()
        pend[0] = 0

    @pl.when(jnp.logical_and(t == 0, f == 0))
    def _():
        xq[...] = jnp.zeros(xq.shape, xq.dtype)
        ystage[...] = jnp.zeros(ystage.shape, ystage.dtype)
        pend[0] = 0
        for part in range(y_tail_rows // tm):
            fill = pltpu.make_async_copy(ystage, y_hbm.at[pl.ds(y_hbm.shape[0] - (part + 1) * tm, tm), :], ssem)
            fill.start()
            fill.wait()

        def invert(g, carry):
            for u in range(DMA_GROUP):
                a = g * DMA_GROUP + u
                asg_ref[pos_ref[a]] = a
            return carry
        lax.fori_loop(0, pos_ref.shape[0] // DMA_GROUP, invert, 0)
        start_gather(0)

    def unpack_rows(m):
        rows = xq[0:m // DMA_GROUP].reshape(m, half)
        lo, hi = _unpack_bf16_pair(rows)
        xb[0, 0:m, :] = lo.astype(BF16)
        xb[1, 0:m, :] = hi.astype(BF16)

    @pl.when(jnp.logical_and(f == 0, n > 0))
    def _():
        wait_gather(n)
        for m in row_counts:
            pl.when(n_granules * MOE_ROW_GRANULE == m)(functools.partial(unpack_rows, m))
        wd0[...] = wd_ref[0].astype(BF16)

    @pl.when(jnp.logical_and(f == 0, t + 1 < n_tiles))
    def _():
        start_gather(t + 1)

    def ffn_rows(m):
        x = xb[f, 0:m, :]
        g_part = jnp.dot(x, wg_ref[0].astype(BF16), preferred_element_type=F32)
        u_part = jnp.dot(x, wu_ref[0].astype(BF16), preferred_element_type=F32)

        @pl.when(f == 0)
        def _():
            gacc[0:m, :] = g_part
            uacc[0:m, :] = u_part

        @pl.when(f == 1)
        def _():
            g = gacc[0:m, :] + g_part
            u = uacc[0:m, :] + u_part
            hid = ((g * jax.nn.sigmoid(g)) * u).astype(BF16)
            fh = wd0.shape[0]
            out = jnp.dot(hid[:, :fh], wd0[...], preferred_element_type=F32)
            out = out + jnp.dot(hid[:, fh:], wd_ref[0].astype(BF16), preferred_element_type=F32)
            wait_scatter()
            ystage[0:m, :] = _pack_bf16_pair(out[:, :half], out[:, half:])
            start_scatter(m)

    for m in row_counts:
        pl.when(n_granules * MOE_ROW_GRANULE == m)(functools.partial(ffn_rows, m))

    @pl.when(jnp.logical_and(f == 1, t == n_tiles - 1))
    def _():
        wait_scatter()


def _moe(xp, tile_e, tile_n, tile_base, pos, n_sorted, w_gate, w_up, w_down, tm):
    assert tm % MOE_ROW_GRANULE == 0 and MOE_ROW_GRANULE % DMA_GROUP == 0 and pos.shape[0] % DMA_GROUP == 0
    t_rows, half = xp.shape
    d = 2 * half
    n_tiles = tile_e.shape[0]
    ff = w_gate.shape[3]
    fh = ff // 2
    y_rows = n_sorted + tm
    y_tail_rows = -(-(y_rows - 2 * t_rows) // tm) * tm

    def fidx(f, tn, t):
        return jnp.where(tn[t] > 0, f, 1)

    grid_spec = pltpu.PrefetchScalarGridSpec(
        num_scalar_prefetch=4,
        grid=(n_tiles, 2),
        in_specs=[pl.BlockSpec(memory_space=pl.ANY),
                  pl.BlockSpec((1, 1, half, ff), lambda t, f, te, tn, tb, ds: (0, te[t], fidx(f, tn, t), 0)),
                  pl.BlockSpec((1, 1, half, ff), lambda t, f, te, tn, tb, ds: (0, te[t], fidx(f, tn, t), 0)),
                  pl.BlockSpec((1, 1, fh, d), lambda t, f, te, tn, tb, ds: (0, te[t], fidx(f, tn, t), 0))],
        out_specs=pl.BlockSpec(memory_space=pl.ANY),
        scratch_shapes=[pltpu.VMEM((tm // DMA_GROUP, DMA_GROUP, half), U32), pltpu.VMEM((2, tm, half), BF16),
                        pltpu.VMEM((tm, ff), F32), pltpu.VMEM((tm, ff), F32), pltpu.VMEM((fh, d), BF16),
                        pltpu.VMEM((tm, half), U32), pltpu.SMEM((n_sorted,), jnp.int32), pltpu.SMEM((1,), jnp.int32),
                        pltpu.SemaphoreType.DMA(()), pltpu.SemaphoreType.DMA(())],
    )
    return pl.pallas_call(
        functools.partial(_moe_kernel_4d, y_tail_rows=y_tail_rows),
        grid_spec=grid_spec,
        out_shape=jax.ShapeDtypeStruct((y_rows, half), U32),
        compiler_params=_params("arbitrary", "arbitrary"),
        name="sparse_moe",
    )(tile_e, tile_n, tile_base, pos, xp, w_gate, w_up, w_down)


def _moe_kernel_4d(te_ref, tn_ref, tb_ref, pos_ref, xp_hbm, wg_ref, wu_ref, wd_ref, *rest, y_tail_rows):
    _moe_kernel(te_ref, tn_ref, tb_ref, pos_ref, xp_hbm, wg_ref.at[0], wu_ref.at[0], wd_ref.at[0], *rest,
                y_tail_rows=y_tail_rows)


def _routing_tables(e_idx, tm, n_tiles):
    flat_e = e_idx.reshape(-1)
    n_assign = flat_e.shape[0]
    onehot = (flat_e[:, None] == jnp.arange(N_EXPERTS, dtype=jnp.int32)[None, :]).astype(jnp.int32)
    csum = jnp.cumsum(onehot, axis=0)
    rank = jnp.sum(csum * onehot, axis=1) - 1
    counts = csum[-1]
    seg = (counts + SUBLANES - 1) // SUBLANES * SUBLANES
    row_end = jnp.cumsum(seg)
    row_start = row_end - seg
    tiles_per_e = (counts + tm - 1) // tm
    tile_end = jnp.cumsum(tiles_per_e)
    tile_start = tile_end - tiles_per_e
    total = tile_end[-1]
    pos = jnp.sum(onehot * row_start[None, :], axis=1) + rank
    n_sorted = n_assign + N_EXPERTS * SUBLANES
    tid = jnp.arange(n_tiles, dtype=jnp.int32)
    te = jnp.sum((tid[:, None] >= tile_end[None, :]).astype(jnp.int32), axis=1)
    te = jnp.minimum(te, N_EXPERTS - 1)
    active = tid < total
    first = (tid - tile_start[te]) * tm
    tn = jnp.where(active, jnp.clip(counts[te] - first, 0, tm), 0).astype(jnp.int32)
    tb = jnp.where(active, row_start[te] + first, 0).astype(jnp.int32)
    last_e = te[jnp.maximum(total - 1, 0)]
    te = jnp.where(active, te, last_e).astype(jnp.int32)
    return te, tn, tb, pos.astype(jnp.int32), n_sorted


def _final_kernel(pos_ref, h_ref, info_ref, g_ref, y_hbm, o_ref, ybuf, sem):
    i = pl.program_id(0)
    half = ybuf.shape[4]
    tm = ybuf.shape[2] * DMA_GROUP
    slot = lax.rem(i, 2)

    def row_copy(buf, k, g, u, src):
        return pltpu.make_async_copy(y_hbm.at[pl.ds(src, 1), :], ybuf.at[buf, k, g, pl.ds(u, 1), :], sem.at[buf])

    def start_gather(step, buf):
        def body(g, carry):
            for u in range(DMA_GROUP):
                for k in range(2):
                    row_copy(buf, k, g, u, pos_ref[2 * (step * tm + g * DMA_GROUP + u) + k]).start()
            return carry
        lax.fori_loop(0, tm // DMA_GROUP, body, 0)

    def wait_gather(buf):
        def body(g, carry):
            for u in range(DMA_GROUP):
                for k in range(2):
                    row_copy(buf, k, g, u, 0).wait()
            return carry
        lax.fori_loop(0, tm // DMA_GROUP, body, 0)

    @pl.when(i == 0)
    def _():
        start_gather(0, 0)

    @pl.when(i + 1 < pl.num_programs(0))
    def _():
        start_gather(i + 1, 1 - slot)

    wait_gather(slot)
    w = info_ref[...]
    w0 = w[:, 0:1]
    w1 = w[:, 1:2]
    y0_lo, y0_hi = _unpack_bf16_pair(ybuf[slot, 0].reshape(tm, half))
    y1_lo, y1_hi = _unpack_bf16_pair(ybuf[slot, 1].reshape(tm, half))
    h_lo = h_ref[:, :half] + w0 * y0_lo + w1 * y1_lo
    h_hi = h_ref[:, half:] + w0 * y0_hi + w1 * y1_hi
    ms = (jnp.sum(h_lo * h_lo, axis=-1, keepdims=True) + jnp.sum(h_hi * h_hi, axis=-1, keepdims=True)) / (2 * half)
    r = lax.rsqrt(ms + RMS_EPS)
    o_ref[:, :half] = h_lo * r * g_ref[:, :half]
    o_ref[:, half:] = h_hi * r * g_ref[:, half:]


def _combine_final(h, y, pos, info, g, tm):
    m, d = h.shape
    assert tm % DMA_GROUP == 0
    grid_spec = pltpu.PrefetchScalarGridSpec(
        num_scalar_prefetch=1,
        grid=(m // tm,),
        in_specs=[pl.BlockSpec((tm, d), lambda i, p: (i, 0)),
                  pl.BlockSpec((tm, LANES), lambda i, p: (i, 0)),
                  pl.BlockSpec((1, d), lambda i, p: (0, 0)),
                  pl.BlockSpec(memory_space=pl.ANY)],
        out_specs=pl.BlockSpec((tm, d), lambda i, p: (i, 0)),
        scratch_shapes=[pltpu.VMEM((2, 2, tm // DMA_GROUP, DMA_GROUP, d // 2), U32), pltpu.SemaphoreType.DMA((2,))],
    )
    return pl.pallas_call(
        _final_kernel,
        grid_spec=grid_spec,
        out_shape=jax.ShapeDtypeStruct((m, d), F32),
        compiler_params=_params("arbitrary"),
        name="combine_final",
    )(pos, h, info, g.reshape(1, d).astype(F32), y)


def kernel(x, meta_tokens, attn_norm_g, w_in, forget_bias, conv_w, conv_b, conv_ln_g, conv_ln_b, w_out,
           ffn_norm_g, w_router_group, b_router_group, w_router_expert, b_router_expert,
           w_expert_gate, w_expert_up, w_expert_down, final_norm_g):
    b, s, d = x.shape
    depth = w_in.shape[0]
    assert depth == 1, "one layer: the meta rows are not carried past the mixer"
    n_heads = forget_bias.shape[1]
    aw = n_heads * HEAD_DIM
    cw = conv_w.shape[2]
    t_rows = b * s
    x2 = x.reshape(t_rows, d)

    xn = _rmsnorm(x2, attn_norm_g[0], BF16, tm=512)
    mn = _rmsnorm(meta_tokens.astype(x.dtype), attn_norm_g[0], BF16, tm=N_META)

    w_t = jnp.transpose(w_in[0])
    qkv, qkv_m = _projection(xn, mn, w_t, 0, 3 * aw, BF16, 1024, 512, "in_proj_qkv")
    fx, f_m = _projection(xn, mn, w_t, 3 * aw, LANES, F32, 1024, LANES, "in_proj_gate")
    u, u_m = _projection(xn, mn, w_t, 3 * aw + n_heads, 2 * cw, BF16, 1024, 512, "in_proj_glu")

    tq = 512
    bias_f = jnp.zeros((1, LANES), F32).at[0, :n_heads].set(forget_bias[0].astype(F32))
    cx, cm = _forget_cumsum(fx.reshape(b, s, LANES), f_m, bias_f)
    cx = jnp.transpose(cx[:, :, :n_heads], (0, 2, 1)).reshape(b, n_heads, s // tq, tq)
    cm = jnp.zeros((n_heads, 1, LANES), F32).at[:, 0, :N_META].set(jnp.transpose(cm[0, :, :n_heads]))
    kvm = jnp.zeros((LANES, 3 * aw), BF16).at[:N_META].set(qkv_m)
    attn = _attention(qkv.reshape(b, s, 3 * aw), kvm, cx, cm, n_heads, tq)

    conv = _conv_branch(u.reshape(b, s, 2 * cw), u_m, conv_w[0], conv_b[0], conv_ln_g[0], conv_ln_b[0], ts=256)

    h1 = _out_projection(attn.reshape(t_rows, aw), conv.reshape(t_rows, cw), w_out, x2, 1024, 512)

    n_r = N_GROUPS + N_EXPERTS
    wr = jnp.pad(jnp.concatenate([w_router_group[0], w_router_expert[0]], axis=1).astype(F32), ((0, 0), (0, LANES - n_r)))
    br = jnp.zeros((1, LANES), F32).at[0, :N_GROUPS].set(b_router_group[0]).at[0, N_GROUPS:n_r].set(b_router_expert[0])
    xp, info = _norm_router(h1, ffn_norm_g[0], wr, br, tm=256)

    tm_moe = 512
    n_tiles = (2 * t_rows) // tm_moe + N_EXPERTS
    te, tn, tb, pos, n_sorted = _routing_tables(info[:, 2:4].astype(jnp.int32), tm_moe, n_tiles)
    y = _moe(xp, te, tn, tb, pos, n_sorted, w_expert_gate, w_expert_up, w_expert_down, tm_moe)

    out = _combine_final(h1, y, pos, info, final_norm_g, tm=256)
    return out.reshape(b, s, d)
```

```python
import functools

import jax
import jax.numpy as jnp
from jax import lax
from jax.experimental import pallas as pl
from jax.experimental.pallas import tpu as pltpu

F32 = jnp.float32
BF16 = jnp.bfloat16
U32 = jnp.uint32

N_META = 16
HEAD_DIM = 128
CONV_KERNEL = 31
N_GROUPS = 8
EXPERTS_PER_GROUP = 8
N_EXPERTS = N_GROUPS * EXPERTS_PER_GROUP
RMS_EPS = 1e-6
LN_EPS = 1e-5

LANES = 128
SUBLANES = 8
VMEM_LIMIT = 56 * 1024 * 1024
CONV_HALO = 32
CONV_ROWS = 256
MOE_ROW_GRANULE = 64
DMA_GROUP = 8


def _params(*sem):
    return pltpu.CompilerParams(dimension_semantics=sem, vmem_limit_bytes=VMEM_LIMIT)


def _pack_bf16_pair(lo, hi):
    lo_bits = lax.bitcast_convert_type(lo.astype(BF16).astype(F32), U32)
    hi_bits = lax.bitcast_convert_type(hi.astype(BF16).astype(F32), U32)
    return (lo_bits >> 16) | hi_bits


def _unpack_bf16_pair(p):
    lo = lax.bitcast_convert_type(p << 16, F32)
    hi = lax.bitcast_convert_type(p & jnp.uint32(0xFFFF0000), F32)
    return lo, hi


def _rmsnorm_kernel(x_ref, g_ref, o_ref):
    x = x_ref[...].astype(F32)
    y = x * lax.rsqrt(jnp.mean(x * x, axis=-1, keepdims=True) + RMS_EPS)
    o_ref[...] = (y * g_ref[...]).astype(o_ref.dtype)


def _rmsnorm(x, g, out_dtype, tm):
    m, d = x.shape
    return pl.pallas_call(
        _rmsnorm_kernel,
        grid=(m // tm,),
        in_specs=[pl.BlockSpec((tm, d), lambda i: (i, 0)), pl.BlockSpec((1, d), lambda i: (0, 0))],
        out_specs=pl.BlockSpec((tm, d), lambda i: (i, 0)),
        out_shape=jax.ShapeDtypeStruct((m, d), out_dtype),
        compiler_params=_params("parallel"),
        name="rmsnorm",
    )(x, g.reshape(1, d).astype(F32))


def _proj_kernel(a_ref, am_ref, wt_ref, o_ref, om_ref, wb_ref):
    i = pl.program_id(1)
    tn, k = wt_ref.shape
    kc = min(k, 512)

    @pl.when(i == 0)
    def _():
        for c in range(k // kc):
            wb_ref[c * kc:(c + 1) * kc, :] = wt_ref[:, c * kc:(c + 1) * kc].T.astype(BF16)
        om_ref[...] = jnp.dot(am_ref[...], wb_ref[...], preferred_element_type=F32).astype(om_ref.dtype)

    o_ref[...] = jnp.dot(a_ref[...], wb_ref[...], preferred_element_type=F32).astype(o_ref.dtype)


def _projection(a, am, wt, row0, n_cols, out_dtype, tm, tn, name):
    m, k = a.shape
    mm = am.shape[0]
    assert n_cols % tn == 0 and m % tm == 0 and row0 % SUBLANES == 0
    if row0 % tn == 0:
        w_spec = pl.BlockSpec((tn, k), lambda j, i: (row0 // tn + j, 0))
    else:
        w_spec = pl.BlockSpec((pl.Element(tn), pl.Element(k)), lambda j, i: ((row0 // SUBLANES + j * (tn // SUBLANES)) * SUBLANES, 0))
    return pl.pallas_call(
        _proj_kernel,
        grid=(n_cols // tn, m // tm),
        in_specs=[pl.BlockSpec((tm, k), lambda j, i: (i, 0)),
                  pl.BlockSpec((mm, k), lambda j, i: (0, 0)),
                  w_spec],
        out_specs=[pl.BlockSpec((tm, tn), lambda j, i: (i, j)), pl.BlockSpec((mm, tn), lambda j, i: (0, j))],
        out_shape=[jax.ShapeDtypeStruct((m, n_cols), out_dtype), jax.ShapeDtypeStruct((mm, n_cols), out_dtype)],
        scratch_shapes=[pltpu.VMEM((k, tn), BF16)],
        compiler_params=_params("parallel", "arbitrary"),
        name=name,
    )(a, am, wt)


def _log_sigmoid(z):
    return jnp.minimum(z, 0.0) - jnp.log1p(jnp.exp(-jnp.abs(z)))


def _split3_bf16(x):
    hi = x.astype(BF16)
    r1 = x - hi.astype(F32)
    mid = r1.astype(BF16)
    lo = (r1 - mid.astype(F32)).astype(BF16)
    return hi, mid, lo


def _cumsum_rows(l, tril):
    dot = functools.partial(jnp.dot, preferred_element_type=F32)
    hi, mid, lo = _split3_bf16(l)
    return dot(tril, hi) + dot(tril, mid) + dot(tril, lo)


def _tril(n):
    r = lax.broadcasted_iota(jnp.int32, (n, n), 0)
    c = lax.broadcasted_iota(jnp.int32, (n, n), 1)
    return jnp.where(r >= c, 1.0, 0.0).astype(BF16)


def _cumsum_kernel(fx_ref, fm_ref, b_ref, cx_ref, cm_ref, *, blk):
    bias = b_ref[...]
    cm = _cumsum_rows(_log_sigmoid(fm_ref[...] + bias), _tril(N_META))
    cm_ref[0] = cm
    carry = cm[N_META - 1:N_META, :]
    tril = _tril(blk)
    for j in range(fx_ref.shape[1] // blk):
        l = _log_sigmoid(fx_ref[0, j * blk:(j + 1) * blk, :] + bias)
        c = _cumsum_rows(l, tril) + carry
        cx_ref[0, j * blk:(j + 1) * blk, :] = c
        carry = c[blk - 1:blk, :]


def _forget_cumsum(fx, fm, bias):
    b, s, _ = fx.shape
    return pl.pallas_call(
        functools.partial(_cumsum_kernel, blk=256),
        grid=(b,),
        in_specs=[pl.BlockSpec((1, s, LANES), lambda i: (i, 0, 0)),
                  pl.BlockSpec((N_META, LANES), lambda i: (0, 0)),
                  pl.BlockSpec((1, LANES), lambda i: (0, 0))],
        out_specs=[pl.BlockSpec((1, s, LANES), lambda i: (i, 0, 0)),
                   pl.BlockSpec((1, N_META, LANES), lambda i: (i, 0, 0))],
        out_shape=[jax.ShapeDtypeStruct((b, s, LANES), F32), jax.ShapeDtypeStruct((b, N_META, LANES), F32)],
        compiler_params=_params("parallel"),
        name="forget_cumsum",
    )(fx, fm, bias)


def _attn_kernel(q_ref, k_ref, v_ref, km_ref, vm_ref, cx_ref, cm_ref, o_ref, *, tq):
    i = pl.program_id(2)
    q = q_ref[0]
    scale = HEAD_DIM ** -0.5

    def scores(kb, cb):
        s = lax.dot_general(q, kb, (((1,), (1,)), ((), ())), preferred_element_type=F32)
        return s * scale - cb

    def partial_softmax(s, vb):
        m = jnp.max(s, axis=-1, keepdims=True)
        p = jnp.exp(s - m)
        return m, jnp.sum(p, axis=-1, keepdims=True), jnp.dot(p.astype(BF16), vb, preferred_element_type=F32)

    def merge(a, b):
        m = jnp.maximum(a[0], b[0])
        wa = jnp.exp(a[0] - m)
        wb = jnp.exp(b[0] - m)
        return m, wa * a[1] + wb * b[1], wa * a[2] + wb * b[2]

    def update(s, vb, m, l, acc):
        m_new = jnp.maximum(m, jnp.max(s, axis=-1, keepdims=True))
        alpha = jnp.exp(m - m_new)
        p = jnp.exp(s - m_new)
        l = alpha * l + jnp.sum(p, axis=-1, keepdims=True)
        acc = alpha * acc + jnp.dot(p.astype(BF16), vb, preferred_element_type=F32)
        return m_new, l, acc

    def block(j):
        off = pl.multiple_of(j * tq, tq)
        return (scores(k_ref[0, pl.ds(off, tq), :], cx_ref[0, 0, pl.ds(j, 1), :]), v_ref[0, pl.ds(off, tq), :])

    def body(j, carry):
        s, vb = block(j)
        return update(s, vb, *carry)

    init = (jnp.full((tq, 1), -jnp.inf, F32), jnp.zeros((tq, 1), F32), jnp.zeros((tq, HEAD_DIM), F32))
    before = lax.fori_loop(0, i, body, init)

    s, vb = block(i)
    row = lax.broadcasted_iota(jnp.int32, s.shape, 0)
    col = lax.broadcasted_iota(jnp.int32, s.shape, 1)
    diag = partial_softmax(jnp.where(row >= col, s, -jnp.inf), vb)
    s = scores(km_ref[...], cm_ref[0])
    col = lax.broadcasted_iota(jnp.int32, s.shape, 1)
    meta = partial_softmax(jnp.where(col < N_META, s, -jnp.inf), vm_ref[...])
    m, l, acc = merge(before, merge(diag, meta))
    o_ref[0] = (acc / l).astype(o_ref.dtype)


def _attention(qkv, kvm, cx, cm, n_heads, tq):
    b, s, _ = qkv.shape
    nk = s // tq
    blk = lambda f: pl.BlockSpec((1, tq, HEAD_DIM), f)
    return pl.pallas_call(
        functools.partial(_attn_kernel, tq=tq),
        grid=(b, n_heads, nk),
        in_specs=[blk(lambda bi, h, i: (bi, i, h)),
                  pl.BlockSpec((1, s, HEAD_DIM), lambda bi, h, i: (bi, 0, n_heads + h)),
                  pl.BlockSpec((1, s, HEAD_DIM), lambda bi, h, i: (bi, 0, 2 * n_heads + h)),
                  pl.BlockSpec((LANES, HEAD_DIM), lambda bi, h, i: (0, n_heads + h)),
                  pl.BlockSpec((LANES, HEAD_DIM), lambda bi, h, i: (0, 2 * n_heads + h)),
                  pl.BlockSpec((1, 1, nk, tq), lambda bi, h, i: (bi, h, 0, 0)),
                  pl.BlockSpec((1, 1, LANES), lambda bi, h, i: (h, 0, 0))],
        out_specs=blk(lambda bi, h, i: (bi, i, h)),
        out_shape=jax.ShapeDtypeStruct((b, s, n_heads * HEAD_DIM), BF16),
        compiler_params=_params("parallel", "parallel", "parallel"),
        name="fox_attention",
    )(qkv, qkv, qkv, kvm, kvm, cx, cm)


def _sigmoid(x):
    return 0.5 * jnp.tanh(0.5 * x) + 0.5


def _conv_kernel(ua_ref, ug_ref, ma_ref, mg_ref, w_ref, cb_ref, g_ref, b_ref, o_ref, hb, hs, cbuf, *, ts):
    t = pl.program_id(1)
    nch = hb.shape[0]
    off = CONV_HALO - (CONV_KERNEL - 1)
    span = ts + CONV_HALO - SUBLANES
    lanes = lambda j: slice(j * LANES, (j + 1) * LANES)

    @pl.when(t == 0)
    def _():
        for j in range(nch):
            hb[j, 0:CONV_HALO - N_META, :] = jnp.zeros((CONV_HALO - N_META, LANES), F32)
            hb[j, CONV_HALO - N_META:CONV_HALO, :] = (
                ma_ref[:, lanes(j)].astype(F32) * _sigmoid(mg_ref[:, lanes(j)].astype(F32)))

    for j in range(nch):
        hb[j, CONV_HALO:CONV_HALO + ts, :] = (
            ua_ref[0, :, lanes(j)].astype(F32) * _sigmoid(ug_ref[0, :, lanes(j)].astype(F32)))

    def lane_chunk(j, carry):
        for s in range(1, SUBLANES):
            hs[s - 1] = hb[j, pl.ds(s, span), :]

        def row_chunk(i, c2):
            r0 = pl.multiple_of(i * CONV_ROWS, CONV_ROWS)
            tile_rows = lambda v: jnp.concatenate([v] * (CONV_ROWS // SUBLANES), axis=0)
            acc = tile_rows(cb_ref[j])
            for k in range(CONV_KERNEL):
                a, s = divmod(off + k, SUBLANES)
                rows = pl.ds(r0 + SUBLANES * a, CONV_ROWS)
                src = hb[j, rows, :] if s == 0 else hs[s - 1, rows, :]
                acc = acc + tile_rows(w_ref[j, k]) * src
            cbuf[j, pl.ds(r0, CONV_ROWS), :] = acc
            return c2

        lax.fori_loop(0, ts // CONV_ROWS, row_chunk, 0)
        return carry

    lax.fori_loop(0, nch, lane_chunk, 0)

    for j in range(nch):
        hb[j, 0:CONV_HALO, :] = hb[j, ts:ts + CONV_HALO, :]

    c = cbuf[...]
    n_ch = nch * LANES
    mu = jnp.sum(jnp.sum(c, axis=0), axis=-1, keepdims=True) / n_ch
    xc = c - mu[None]
    var = jnp.sum(jnp.sum(xc * xc, axis=0), axis=-1, keepdims=True) / n_ch
    y = xc * lax.rsqrt(var + LN_EPS)[None] * g_ref[...] + b_ref[...]
    y = y * _sigmoid(y)
    for j in range(nch):
        o_ref[0, :, lanes(j)] = y[j].astype(o_ref.dtype)


def _conv_branch(u, um, conv_w, conv_b, ln_g, ln_b, ts):
    b, s, c2 = u.shape
    c = c2 // 2
    nch = c // LANES
    chunked = lambda v: jnp.transpose(v.astype(F32).reshape(-1, nch, LANES), (1, 0, 2))
    w_rep = jnp.broadcast_to(chunked(conv_w)[:, :, None, :], (nch, CONV_KERNEL, SUBLANES, LANES))
    cb_rep = jnp.broadcast_to(chunked(conv_b), (nch, SUBLANES, LANES))
    full = lambda shape: pl.BlockSpec(shape, lambda bi, t: (0,) * len(shape))
    return pl.pallas_call(
        functools.partial(_conv_kernel, ts=ts),
        grid=(b, s // ts),
        in_specs=[pl.BlockSpec((1, ts, c), lambda bi, t: (bi, t, 0)),
                  pl.BlockSpec((1, ts, c), lambda bi, t: (bi, t, 1)),
                  pl.BlockSpec((N_META, c), lambda bi, t: (0, 0)),
                  pl.BlockSpec((N_META, c), lambda bi, t: (0, 1)),
                  full((nch, CONV_KERNEL, SUBLANES, LANES)),
                  full((nch, SUBLANES, LANES)),
                  full((nch, 1, LANES)),
                  full((nch, 1, LANES))],
        out_specs=pl.BlockSpec((1, ts, c), lambda bi, t: (bi, t, 0)),
        out_shape=jax.ShapeDtypeStruct((b, s, c), BF16),
        scratch_shapes=[pltpu.VMEM((nch, ts + CONV_HALO, LANES), F32),
                        pltpu.VMEM((SUBLANES - 1, ts + CONV_HALO - SUBLANES, LANES), F32),
                        pltpu.VMEM((nch, ts, LANES), F32)],
        compiler_params=_params("parallel", "arbitrary"),
        name="conformer_conv",
    )(u, u, um, um, w_rep, cb_rep, chunked(ln_g), chunked(ln_b))


def _outproj_kernel(a_ref, c_ref, wa_ref, wc_ref, x_ref, o_ref, wab_ref, wcb_ref):
    @pl.when(pl.program_id(1) == 0)
    def _():
        wab_ref[...] = wa_ref[0].astype(BF16)
        wcb_ref[...] = wc_ref[0].astype(BF16)

    acc = jnp.dot(a_ref[...], wab_ref[...], preferred_element_type=F32)
    acc = acc + jnp.dot(c_ref[...], wcb_ref[...], preferred_element_type=F32)
    o_ref[...] = x_ref[...] + acc


def _out_projection(a, c, w, x, tm, tn):
    m, ka = a.shape
    kc = c.shape[1]
    n = w.shape[2]
    assert ka == kc
    return pl.pallas_call(
        _outproj_kernel,
        grid=(n // tn, m // tm),
        in_specs=[pl.BlockSpec((tm, ka), lambda j, i: (i, 0)),
                  pl.BlockSpec((tm, kc), lambda j, i: (i, 0)),
                  pl.BlockSpec((1, ka, tn), lambda j, i: (0, 0, j)),
                  pl.BlockSpec((1, kc, tn), lambda j, i: (0, 1, j)),
                  pl.BlockSpec((tm, tn), lambda j, i: (i, j))],
        out_specs=pl.BlockSpec((tm, tn), lambda j, i: (i, j)),
        out_shape=jax.ShapeDtypeStruct((m, n), F32),
        scratch_shapes=[pltpu.VMEM((ka, tn), BF16), pltpu.VMEM((kc, tn), BF16)],
        compiler_params=_params("parallel", "arbitrary"),
        name="out_projection",
    )(a, c, w, w, x)


def _router_kernel(h_ref, g_ref, wr_ref, br_ref, xp_ref, info_ref):
    h = h_ref[...]
    xn = h * lax.rsqrt(jnp.mean(h * h, axis=-1, keepdims=True) + RMS_EPS) * g_ref[...]
    half = xn.shape[1] // 2
    xp_ref[...] = _pack_bf16_pair(xn[:, :half], xn[:, half:])
    logits = jnp.dot(xn, wr_ref[...], precision=lax.Precision.HIGHEST, preferred_element_type=F32) + br_ref[...]
    lane = lax.broadcasted_iota(jnp.int32, logits.shape, 1).astype(F32)
    rmax = lambda v: jnp.max(v, axis=-1, keepdims=True)
    rmin = lambda v: jnp.min(v, axis=-1, keepdims=True)
    rsum = lambda v: jnp.sum(v, axis=-1, keepdims=True)
    far = float(LANES)

    is_g = lane < N_GROUPS
    lg = jnp.where(is_g, logits, -jnp.inf)
    mg = rmax(lg)
    g_idx = rmin(jnp.where(lg == mg, lane, far))
    g_w = 1.0 / rsum(jnp.where(is_g, jnp.exp(lg - mg), 0.0))

    lo = N_GROUPS + EXPERTS_PER_GROUP * g_idx
    in_e = jnp.logical_and(lane >= lo, lane < lo + EXPERTS_PER_GROUP)
    le = jnp.where(in_e, logits, -jnp.inf)
    pe = jnp.where(in_e, jnp.exp(le - rmax(le)), 0.0)
    prob = jnp.where(in_e, pe / rsum(pe), -1.0)
    p1 = rmax(prob)
    i1 = rmin(jnp.where(prob == p1, lane, far))
    prob2 = jnp.where(lane == i1, -1.0, prob)
    p2 = rmax(prob2)
    i2 = rmin(jnp.where(prob2 == p2, lane, far))
    den = p1 + p2
    w1 = g_w * (p1 / den)
    w2 = g_w * (p2 / den)
    info = jnp.where(lane == 0, w1, jnp.where(lane == 1, w2, jnp.where(
        lane == 2, i1 - N_GROUPS, jnp.where(lane == 3, i2 - N_GROUPS, 0.0))))
    info_ref[...] = info


def _norm_router(h, g, wr, br, tm):
    m, d = h.shape
    return pl.pallas_call(
        _router_kernel,
        grid=(m // tm,),
        in_specs=[pl.BlockSpec((tm, d), lambda i: (i, 0)),
                  pl.BlockSpec((1, d), lambda i: (0, 0)),
                  pl.BlockSpec((d, LANES), lambda i: (0, 0)),
                  pl.BlockSpec((1, LANES), lambda i: (0, 0))],
        out_specs=[pl.BlockSpec((tm, d // 2), lambda i: (i, 0)), pl.BlockSpec((tm, LANES), lambda i: (i, 0))],
        out_shape=[jax.ShapeDtypeStruct((m, d // 2), U32), jax.ShapeDtypeStruct((m, LANES), F32)],
        compiler_params=_params("parallel"),
        name="norm_router",
    )(h, g.reshape(1, d).astype(F32), wr, br)


def _moe_kernel(te_ref, tn_ref, tb_ref, pos_ref, xp_hbm, wg_ref, wu_ref, wd_ref, y_hbm,
                xq, xb, gacc, uacc, wd0, ystage, asg_ref, pend, gsem, ssem, *, y_tail_rows):
    t = pl.program_id(0)
    f = pl.program_id(1)
    n_tiles = pl.num_programs(0)
    n = tn_ref[t]
    tm, half = ystage.shape
    groups = lambda cnt: (cnt + DMA_GROUP - 1) // DMA_GROUP

    def gather_copy(g, u, tok):
        return pltpu.make_async_copy(xp_hbm.at[pl.ds(tok, 1), :], xq.at[g, pl.ds(u, 1), :], gsem)

    def start_gather(tile):
        base = tb_ref[tile]
        cnt = tn_ref[tile]

        def body(g, carry):
            for u in range(DMA_GROUP):
                r = g * DMA_GROUP + u
                gather_copy(g, u, asg_ref[base + jnp.minimum(r, cnt - 1)] >> 1).start()
            return carry
        lax.fori_loop(0, groups(cnt), body, 0)

    def wait_gather(cnt):
        def body(g, carry):
            for u in range(DMA_GROUP):
                gather_copy(g, u, 0).wait()
            return carry
        lax.fori_loop(0, groups(cnt), body, 0)

    n_granules = (n + MOE_ROW_GRANULE - 1) // MOE_ROW_GRANULE
    row_counts = [k * MOE_ROW_GRANULE for k in range(1, tm // MOE_ROW_GRANULE + 1)]

    def out_copy(m, base):
        return pltpu.make_async_copy(ystage.at[pl.ds(0, m), :], y_hbm.at[pl.ds(base, m), :], ssem)

    def start_scatter(m):
        out_copy(m, pl.multiple_of(tb_ref[t], SUBLANES)).start()
        pend[0] = m

    def wait_scatter():
        for m in row_counts:
            @pl.when(pend[0] == m)
            def _(m=m):
                out_copy(m, 0).wait()
        pend[0] = 0

    @pl.when(jnp.logical_and(t == 0, f == 0))
    def _():
        xq[...] = jnp.zeros(xq.shape, xq.dtype)
        ystage[...] = jnp.zeros(ystage.shape, ystage.dtype)
        pend[0] = 0
        for part in range(y_tail_rows // tm):
            fill = pltpu.make_async_copy(ystage, y_hbm.at[pl.ds(y_hbm.shape[0] - (part + 1) * tm, tm), :], ssem)
            fill.start()
            fill.wait()

        def invert(g, carry):
            for u in range(DMA_GROUP):
                a = g * DMA_GROUP + u
                asg_ref[pos_ref[a]] = a
            return carry
        lax.fori_loop(0, pos_ref.shape[0] // DMA_GROUP, invert, 0)
        start_gather(0)

    def unpack_rows(m):
        rows = xq[0:m // DMA_GROUP].reshape(m, half)
        lo, hi = _unpack_bf16_pair(rows)
        xb[0, 0:m, :] = lo.astype(BF16)
        xb[1, 0:m, :] = hi.astype(BF16)

    @pl.when(jnp.logical_and(f == 0, n > 0))
    def _():
        wait_gather(n)
        for m in row_counts:
            pl.when(n_granules * MOE_ROW_GRANULE == m)(functools.partial(unpack_rows, m))
        wd0[...] = wd_ref[0].astype(BF16)

    @pl.when(jnp.logical_and(f == 0, t + 1 < n_tiles))
    def _():
        start_gather(t + 1)

    def ffn_rows(m):
        x = xb[f, 0:m, :]
        g_part = jnp.dot(x, wg_ref[0].astype(BF16), preferred_element_type=F32)
        u_part = jnp.dot(x, wu_ref[0].astype(BF16), preferred_element_type=F32)

        @pl.when(f == 0)
        def _():
            gacc[0:m, :] = g_part
            uacc[0:m, :] = u_part

        @pl.when(f == 1)
        def _():
            g = gacc[0:m, :] + g_part
            u = uacc[0:m, :] + u_part
            hid = ((g * jax.nn.sigmoid(g)) * u).astype(BF16)
            fh = wd0.shape[0]
            out = jnp.dot(hid[:, :fh], wd0[...], preferred_element_type=F32)
            out = out + jnp.dot(hid[:, fh:], wd_ref[0].astype(BF16), preferred_element_type=F32)
            wait_scatter()
            ystage[0:m, :] = _pack_bf16_pair(out[:, :half], out[:, half:])
            start_scatter(m)

    for m in row_counts:
        pl.when(n_granules * MOE_ROW_GRANULE == m)(functools.partial(ffn_rows, m))

    @pl.when(jnp.logical_and(f == 1, t == n_tiles - 1))
    def _():
        wait_scatter()


def _moe(xp, tile_e, tile_n, tile_base, pos, n_sorted, w_gate, w_up, w_down, tm):
    assert tm % MOE_ROW_GRANULE == 0 and MOE_ROW_GRANULE % DMA_GROUP == 0 and pos.shape[0] % DMA_GROUP == 0
    t_rows, half = xp.shape
    d = 2 * half
    n_tiles = tile_e.shape[0]
    ff = w_gate.shape[3]
    fh = ff // 2
    y_rows = n_sorted + tm
    y_tail_rows = -(-(y_rows - 2 * t_rows) // tm) * tm

    def fidx(f, tn, t):
        return jnp.where(tn[t] > 0, f, 1)

    grid_spec = pltpu.PrefetchScalarGridSpec(
        num_scalar_prefetch=4,
        grid=(n_tiles, 2),
        in_specs=[pl.BlockSpec(memory_space=pl.ANY),
                  pl.BlockSpec((1, 1, half, ff), lambda t, f, te, tn, tb, ds: (0, te[t], fidx(f, tn, t), 0)),
                  pl.BlockSpec((1, 1, half, ff), lambda t, f, te, tn, tb, ds: (0, te[t], fidx(f, tn, t), 0)),
                  pl.BlockSpec((1, 1, fh, d), lambda t, f, te, tn, tb, ds: (0, te[t], fidx(f, tn, t), 0))],
        out_specs=pl.BlockSpec(memory_space=pl.ANY),
        scratch_shapes=[pltpu.VMEM((tm // DMA_GROUP, DMA_GROUP, half), U32), pltpu.VMEM((2, tm, half), BF16),
                        pltpu.VMEM((tm, ff), F32), pltpu.VMEM((tm, ff), F32), pltpu.VMEM((fh, d), BF16),
                        pltpu.VMEM((tm, half), U32), pltpu.SMEM((n_sorted,), jnp.int32), pltpu.SMEM((1,), jnp.int32),
                        pltpu.SemaphoreType.DMA(()), pltpu.SemaphoreType.DMA(())],
    )
    return pl.pallas_call(
        functools.partial(_moe_kernel_4d, y_tail_rows=y_tail_rows),
        grid_spec=grid_spec,
        out_shape=jax.ShapeDtypeStruct((y_rows, half), U32),
        compiler_params=_params("arbitrary", "arbitrary"),
        name="sparse_moe",
    )(tile_e, tile_n, tile_base, pos, xp, w_gate, w_up, w_down)


def _moe_kernel_4d(te_ref, tn_ref, tb_ref, pos_ref, xp_hbm, wg_ref, wu_ref, wd_ref, *rest, y_tail_rows):
    _moe_kernel(te_ref, tn_ref, tb_ref, pos_ref, xp_hbm, wg_ref.at[0], wu_ref.at[0], wd_ref.at[0], *rest,
                y_tail_rows=y_tail_rows)


def _routing_tables(e_idx, tm, n_tiles):
    flat_e = e_idx.reshape(-1)
    n_assign = flat_e.shape[0]
    onehot = (flat_e[:, None] == jnp.arange(N_EXPERTS, dtype=jnp.int32)[None, :]).astype(jnp.int32)
    csum = jnp.cumsum(onehot, axis=0)
    rank = jnp.sum(csum * onehot, axis=1) - 1
    counts = csum[-1]
    seg = (counts + SUBLANES - 1) // SUBLANES * SUBLANES
    row_end = jnp.cumsum(seg)
    row_start = row_end - seg
    tiles_per_e = (counts + tm - 1) // tm
    tile_end = jnp.cumsum(tiles_per_e)
    tile_start = tile_end - tiles_per_e
    total = tile_end[-1]
    pos = jnp.sum(onehot * row_start[None, :], axis=1) + rank
    n_sorted = n_assign + N_EXPERTS * SUBLANES
    tid = jnp.arange(n_tiles, dtype=jnp.int32)
    te = jnp.sum((tid[:, None] >= tile_end[None, :]).astype(jnp.int32), axis=1)
    te = jnp.minimum(te, N_EXPERTS - 1)
    active = tid < total
    first = (tid - tile_start[te]) * tm
    tn = jnp.where(active, jnp.clip(counts[te] - first, 0, tm), 0).astype(jnp.int32)
    tb = jnp.where(active, row_start[te] + first, 0).astype(jnp.int32)
    last_e = te[jnp.maximum(total - 1, 0)]
    te = jnp.where(active, te, last_e).astype(jnp.int32)
    return te, tn, tb, pos.astype(jnp.int32), n_sorted


def _final_kernel(pos_ref, h_ref, info_ref, g_ref, y_hbm, o_ref, ybuf, sem):
    i = pl.program_id(0)
    half = ybuf.shape[4]
    tm = ybuf.shape[2] * DMA_GROUP
    slot = lax.rem(i, 2)

    def row_copy(buf, k, g, u, src):
        return pltpu.make_async_copy(y_hbm.at[pl.ds(src, 1), :], ybuf.at[buf, k, g, pl.ds(u, 1), :], sem.at[buf])

    def start_gather(step, buf):
        def body(g, carry):
            for u in range(DMA_GROUP):
                for k in range(2):
                    row_copy(buf, k, g, u, pos_ref[2 * (step * tm + g * DMA_GROUP + u) + k]).start(priority=k)
            return carry
        lax.fori_loop(0, tm // DMA_GROUP, body, 0)

    def wait_gather(buf):
        def body(g, carry):
            for u in range(DMA_GROUP):
                for k in range(2):
                    row_copy(buf, k, g, u, 0).wait()
            return carry
        lax.fori_loop(0, tm // DMA_GROUP, body, 0)

    @pl.when(i == 0)
    def _():
        start_gather(0, 0)

    @pl.when(i + 1 < pl.num_programs(0))
    def _():
        start_gather(i + 1, 1 - slot)

    wait_gather(slot)
    w = info_ref[...]
    w0 = w[:, 0:1]
    w1 = w[:, 1:2]
    y0_lo, y0_hi = _unpack_bf16_pair(ybuf[slot, 0].reshape(tm, half))
    y1_lo, y1_hi = _unpack_bf16_pair(ybuf[slot, 1].reshape(tm, half))
    h_lo = h_ref[:, :half] + w0 * y0_lo + w1 * y1_lo
    h_hi = h_ref[:, half:] + w0 * y0_hi + w1 * y1_hi
    ms = (jnp.sum(h_lo * h_lo, axis=-1, keepdims=True) + jnp.sum(h_hi * h_hi, axis=-1, keepdims=True)) / (2 * half)
    r = lax.rsqrt(ms + RMS_EPS)
    o_ref[:, :half] = h_lo * r * g_ref[:, :half]
    o_ref[:, half:] = h_hi * r * g_ref[:, half:]


def _combine_final(h, y, pos, info, g, tm):
    m, d = h.shape
    assert tm % DMA_GROUP == 0
    grid_spec = pltpu.PrefetchScalarGridSpec(
        num_scalar_prefetch=1,
        grid=(m // tm,),
        in_specs=[pl.BlockSpec((tm, d), lambda i, p: (i, 0)),
                  pl.BlockSpec((tm, LANES), lambda i, p: (i, 0)),
                  pl.BlockSpec((1, d), lambda i, p: (0, 0)),
                  pl.BlockSpec(memory_space=pl.ANY)],
        out_specs=pl.BlockSpec((tm, d), lambda i, p: (i, 0)),
        scratch_shapes=[pltpu.VMEM((2, 2, tm // DMA_GROUP, DMA_GROUP, d // 2), U32), pltpu.SemaphoreType.DMA((2,))],
    )
    return pl.pallas_call(
        _final_kernel,
        grid_spec=grid_spec,
        out_shape=jax.ShapeDtypeStruct((m, d), F32),
        compiler_params=_params("arbitrary"),
        name="combine_final",
    )(pos, h, info, g.reshape(1, d).astype(F32), y)


def kernel(x, meta_tokens, attn_norm_g, w_in, forget_bias, conv_w, conv_b, conv_ln_g, conv_ln_b, w_out,
           ffn_norm_g, w_router_group, b_router_group, w_router_expert, b_router_expert,
           w_expert_gate, w_expert_up, w_expert_down, final_norm_g):
    b, s, d = x.shape
    depth = w_in.shape[0]
    assert depth == 1, "one layer: the meta rows are not carried past the mixer"
    n_heads = forget_bias.shape[1]
    aw = n_heads * HEAD_DIM
    cw = conv_w.shape[2]
    t_rows = b * s
    x2 = x.reshape(t_rows, d)

    xn = _rmsnorm(x2, attn_norm_g[0], BF16, tm=512)
    mn = _rmsnorm(meta_tokens.astype(x.dtype), attn_norm_g[0], BF16, tm=N_META)

    w_t = jnp.transpose(w_in[0])
    qkv, qkv_m = _projection(xn, mn, w_t, 0, 3 * aw, BF16, 1024, 512, "in_proj_qkv")
    fx, f_m = _projection(xn, mn, w_t, 3 * aw, LANES, F32, 1024, LANES, "in_proj_gate")
    u, u_m = _projection(xn, mn, w_t, 3 * aw + n_heads, 2 * cw, BF16, 1024, 512, "in_proj_glu")

    tq = 512
    bias_f = jnp.zeros((1, LANES), F32).at[0, :n_heads].set(forget_bias[0].astype(F32))
    cx, cm = _forget_cumsum(fx.reshape(b, s, LANES), f_m, bias_f)
    cx = jnp.transpose(cx[:, :, :n_heads], (0, 2, 1)).reshape(b, n_heads, s // tq, tq)
    cm = jnp.zeros((n_heads, 1, LANES), F32).at[:, 0, :N_META].set(jnp.transpose(cm[0, :, :n_heads]))
    kvm = jnp.zeros((LANES, 3 * aw), BF16).at[:N_META].set(qkv_m)
    attn = _attention(qkv.reshape(b, s, 3 * aw), kvm, cx, cm, n_heads, tq)

    conv = _conv_branch(u.reshape(b, s, 2 * cw), u_m, conv_w[0], conv_b[0], conv_ln_g[0], conv_ln_b[0], ts=256)

    h1 = _out_projection(attn.reshape(t_rows, aw), conv.reshape(t_rows, cw), w_out, x2, 1024, 512)

    n_r = N_GROUPS + N_EXPERTS
    wr = jnp.pad(jnp.concatenate([w_router_group[0], w_router_expert[0]], axis=1).astype(F32), ((0, 0), (0, LANES - n_r)))
    br = jnp.zeros((1, LANES), F32).at[0, :N_GROUPS].set(b_router_group[0]).at[0, N_GROUPS:n_r].set(b_router_expert[0])
    xp, info = _norm_router(h1, ffn_norm_g[0], wr, br, tm=256)

    tm_moe = 512
    n_tiles = (2 * t_rows) // tm_moe + N_EXPERTS
    te, tn, tb, pos, n_sorted = _routing_tables(info[:, 2:4].astype(jnp.int32), tm_moe, n_tiles)
    y = _moe(xp, te, tn, tb, pos, n_sorted, w_expert_gate, w_expert_up, w_expert_down, tm_moe)

    out = _combine_final(h1, y, pos, info, final_norm_g, tm=256)
    return out.reshape(b, s, d)
```
